```python
import math
import jax, jax.numpy as jnp
from jax import lax
import numpy as np

D_MODEL = 1024
BATCH = 8
SEQ = 4096
DEPTH = 4

GDN_HEADS = 8
GDN_DK = 128
GDN_DV = 128
GDN_QK = GDN_HEADS * GDN_DK
GDN_WIDTH = GDN_HEADS * GDN_DV
GDN_CONV_CH = 2 * GDN_QK + GDN_WIDTH
CONV_WIDTH = 5
CHUNK = 64
DIFF_HEADS = 8
DIFF_DH = 64
DIFF_DV = 2 * DIFF_DH
DIFF_QK = DIFF_HEADS * 2 * DIFF_DH
DIFF_WIDTH = DIFF_HEADS * DIFF_DV
Q_BLOCK = 128
ROPE_THETA = 500000.0
ROPE_DIM = DIFF_DH // 4
EPS = 1e-6

_SEG = (GDN_QK, GDN_QK, GDN_WIDTH, 4 * GDN_HEADS, GDN_WIDTH,
        DIFF_QK, DIFF_QK, DIFF_WIDTH, DIFF_WIDTH, D_MODEL, D_MODEL)
N_IN = sum(_SEG)
SPLIT_IDX = tuple(sum(_SEG[:i + 1]) for i in range(len(_SEG) - 1))

kernel_name = "hybrid_gdn_diffattn_gated_merge_encoder"


def rmsnorm(x, w):
    xf = x.astype(jnp.float32)
    y = xf * lax.rsqrt(jnp.mean(xf * xf, axis=-1, keepdims=True) + EPS)
    return y.astype(x.dtype) * w


def l2norm(x):
    return x * lax.rsqrt(jnp.sum(x * x, axis=-1, keepdims=True) + EPS)


def short_conv(x, w):
    pad = CONV_WIDTH // 2
    return lax.conv_general_dilated(
        x, w[:, None, :].astype(x.dtype), window_strides=(1,), padding=[(pad, pad)],
        dimension_numbers=("NWC", "WIO", "NWC"), feature_group_count=x.shape[-1])


def chunk_gated_delta(q, k, v, g, beta):
    f32 = jnp.float32
    q, k, v, g, beta = (t.astype(f32) for t in (q, k, v, g, beta))
    bsz, nh, seq, dk = k.shape
    dv = v.shape[-1]
    nc = seq // CHUNK
    q = q * dk ** -0.5
    kb = k * beta[..., None]
    vb = v * beta[..., None]
    rs = lambda t: t.reshape(bsz, nh, nc, CHUNK, t.shape[-1])
    q, k, kb, vb = rs(q), rs(k), rs(kb), rs(vb)
    g = jnp.cumsum(g.reshape(bsz, nh, nc, CHUNK), axis=-1)
    lower = jnp.tril(jnp.ones((CHUNK, CHUNK), dtype=bool))
    strict = jnp.tril(jnp.ones((CHUNK, CHUNK), dtype=bool), -1)
    gdiff = g[..., :, None] - g[..., None, :]
    decay = jnp.where(lower, jnp.exp(jnp.where(lower, gdiff, 0.0)), 0.0)
    L = jnp.where(strict, jnp.einsum('bhncd,bhnmd->bhncm', kb, k) * decay, 0.0)
    eye = jnp.eye(CHUNK, dtype=f32)
    T = lax.linalg.triangular_solve(eye + L, jnp.broadcast_to(eye, L.shape),
                                    left_side=True, lower=True, unit_diagonal=True)
    u = jnp.einsum('bhncm,bhnme->bhnce', T, vb)
    w = jnp.einsum('bhncm,bhnmd->bhncd', T, kb * jnp.exp(g)[..., None])
    qk = jnp.where(lower, jnp.einsum('bhncd,bhnmd->bhncm', q, k) * decay, 0.0)

    def step(state, inp):
        q_c, k_c, u_c, w_c, g_c, qk_c = inp
        v_new = u_c - jnp.einsum('bhcd,bhde->bhce', w_c, state)
        o = (jnp.einsum('bhcd,bhde->bhce', q_c * jnp.exp(g_c)[..., None], state)
             + jnp.einsum('bhcm,bhme->bhce', qk_c, v_new))
        g_last = g_c[..., -1]
        state = (state * jnp.exp(g_last)[..., None, None]
                 + jnp.einsum('bhcd,bhce->bhde', k_c * jnp.exp(g_last[..., None] - g_c)[..., None], v_new))
        return state, o

    xs = tuple(jnp.moveaxis(t, 2, 0) for t in (q, k, u, w, g, qk))
    state0 = jnp.zeros((bsz, nh, dk, dv), f32)
    _, o = lax.scan(step, state0, xs)
    return jnp.moveaxis(o, 0, 2).reshape(bsz, nh, seq, dv)


def gated_delta_branch(q, k, v, ab, z, conv_w, a_log, dt_bias, norm_w):
    bsz, seq, _ = q.shape
    f32 = jnp.float32
    qkv = jax.nn.silu(short_conv(jnp.concatenate([q, k, v], axis=-1), conv_w))
    q, k, v = jnp.split(qkv, (GDN_QK, 2 * GDN_QK), axis=-1)
    heads = lambda t, d: t.reshape(bsz, seq, GDN_HEADS, d).transpose(0, 2, 1, 3).astype(f32)
    q = l2norm(heads(q, GDN_DK))
    k = l2norm(heads(k, GDN_DK))
    v = heads(v, GDN_DV)
    ab = ab.reshape(bsz, seq, 4, GDN_HEADS).transpose(2, 0, 3, 1).astype(f32)
    beta = jax.nn.sigmoid(ab[:2])
    g = (-jnp.exp(a_log.astype(f32))[:, None, :, None]
         * jax.nn.softplus(ab[2:] + dt_bias.astype(f32)[:, None, :, None]))
    o_fwd = chunk_gated_delta(q, k, v, g[0], beta[0])
    flip = lambda t: jnp.flip(t, axis=2)
    o_bwd = flip(chunk_gated_delta(flip(q), flip(k), flip(v), flip(g[1]), flip(beta[1])))
    o = rmsnorm(o_fwd + o_bwd, norm_w.astype(f32))
    o = o.transpose(0, 2, 1, 3).reshape(bsz, seq, GDN_WIDTH).astype(z.dtype)
    return o * jax.nn.silu(z)


def partial_rope(t, cos, sin):
    half = ROPE_DIM // 2
    c = cos[:, :, None, None, :].astype(t.dtype)
    s = sin[:, :, None, None, :].astype(t.dtype)
    x1 = t[..., :half]
    x2 = t[..., half:ROPE_DIM]
    return jnp.concatenate([x1 * c - x2 * s, x2 * c + x1 * s, t[..., ROPE_DIM:]], axis=-1)


def diff_attention_branch(q, k, v, z, cos, sin, lam_params, subln_w, lambda_init):
    bsz, seq, _ = q.shape
    f32 = jnp.float32
    q = partial_rope(q.reshape(bsz, seq, DIFF_HEADS, 2, DIFF_DH), cos, sin) * (DIFF_DH ** -0.5)
    k = partial_rope(k.reshape(bsz, seq, DIFF_HEADS, 2, DIFF_DH), cos, sin)
    q = q.transpose(3, 0, 2, 1, 4)
    k = k.transpose(3, 0, 2, 1, 4)
    v = v.reshape(bsz, seq, DIFF_HEADS, DIFF_DV).transpose(0, 2, 1, 3)
    lp = lam_params.astype(f32)
    lam = jnp.exp(jnp.sum(lp[0] * lp[1])) - jnp.exp(jnp.sum(lp[2] * lp[3])) + lambda_init
    n_blk = seq // Q_BLOCK
    qb = q.reshape(2, bsz, DIFF_HEADS, n_blk, Q_BLOCK, DIFF_DH).transpose(3, 0, 1, 2, 4, 5)

    def attend(q_blk):
        s = jnp.einsum('nbhqd,nbhkd->nbhqk', q_blk, k).astype(f32)
        p = jax.nn.softmax(s, axis=-1)
        a = p[0] - lam * p[1]
        return jnp.einsum('bhqk,bhkd->bhqd', a.astype(v.dtype), v)

    o = lax.map(attend, qb)
    o = o.transpose(1, 2, 0, 3, 4).reshape(bsz, DIFF_HEADS, seq, DIFF_DV)
    o = rmsnorm(o, subln_w) * (1.0 - lambda_init)
    o = o.transpose(0, 2, 1, 3).reshape(bsz, seq, DIFF_WIDTH)
    return o * jax.nn.silu(z)


def setup_inputs(seed: int = 0) -> dict:
    key = jax.random.key(seed)
    ks = jax.random.split(key, 16)
    f32 = jnp.float32
    x = jax.random.normal(ks[0], (BATCH, SEQ, D_MODEL), f32)
    positions = (jnp.arange(SEQ, dtype=jnp.int32)[None, :]
                 + jax.random.randint(ks[1], (BATCH, 1), 0, 4096, dtype=jnp.int32))
    norm_w = 1.0 + 0.02 * jax.random.normal(ks[2], (DEPTH, D_MODEL), f32)
    w_in = jax.random.normal(ks[3], (DEPTH, D_MODEL, N_IN), f32) * D_MODEL ** -0.5
    conv_w = jax.random.normal(ks[4], (DEPTH, CONV_WIDTH, GDN_CONV_CH), f32) * CONV_WIDTH ** -0.5
    a_log = jnp.log(jax.random.uniform(ks[5], (DEPTH, 2, GDN_HEADS), f32, 1.0, 16.0))
    dt = jnp.exp(jax.random.uniform(ks[6], (DEPTH, 2, GDN_HEADS), f32, math.log(1e-3), math.log(1e-1)))
    dt_bias = dt + jnp.log(-jnp.expm1(-dt))
    gdn_norm_w = 1.0 + 0.02 * jax.random.normal(ks[7], (DEPTH, GDN_DV), f32)
    diff_lambda = 0.1 * jax.random.normal(ks[8], (DEPTH, 4, DIFF_DH), f32)
    diff_subln_w = 1.0 + 0.02 * jax.random.normal(ks[9], (DEPTH, DIFF_DV), f32)
    w_pa = jax.random.normal(ks[10], (DEPTH, GDN_WIDTH, D_MODEL), f32) * GDN_WIDTH ** -0.5
    w_pb = jax.random.normal(ks[11], (DEPTH, DIFF_WIDTH, D_MODEL), f32) * DIFF_WIDTH ** -0.5
    w_out = jax.random.normal(ks[12], (DEPTH, D_MODEL, D_MODEL), f32) * D_MODEL ** -0.5
    final_norm_w = 1.0 + 0.02 * jax.random.normal(ks[13], (D_MODEL,), f32)
    return {"x": x, "positions": positions, "norm_w": norm_w, "w_in": w_in, "conv_w": conv_w,
            "a_log": a_log, "dt_bias": dt_bias, "gdn_norm_w": gdn_norm_w,
            "diff_lambda": diff_lambda, "diff_subln_w": diff_subln_w, "w_pa": w_pa,
            "w_pb": w_pb, "w_out": w_out, "final_norm_w": final_norm_w}


def reference(x, positions, norm_w, w_in, conv_w, a_log, dt_bias, gdn_norm_w,
              diff_lambda, diff_subln_w, w_pa, w_pb, w_out, final_norm_w):
    inv_freq = ROPE_THETA ** (-(jnp.arange(0, ROPE_DIM, 2, dtype=jnp.float32) / ROPE_DIM))
    angles = positions.astype(jnp.float32)[..., None] * inv_freq
    cos, sin = jnp.cos(angles), jnp.sin(angles)
    for l in range(DEPTH):
        h = rmsnorm(x, norm_w[l])
        proj = h @ w_in[l]
        (q_a, k_a, v_a, ab_a, z_a, q_b, k_b, v_b, z_b,
         gate_a, gate_b) = jnp.split(proj, SPLIT_IDX, axis=-1)
        y_a = gated_delta_branch(q_a, k_a, v_a, ab_a, z_a, conv_w[l], a_log[l], dt_bias[l], gdn_norm_w[l])
        lambda_init = 0.8 - 0.6 * math.exp(-0.3 * l)
        y_b = diff_attention_branch(q_b, k_b, v_b, z_b, cos, sin, diff_lambda[l], diff_subln_w[l], lambda_init)
        merged = (jax.nn.sigmoid(gate_a) * (y_a @ w_pa[l])
                  + jax.nn.sigmoid(gate_b) * (y_b @ w_pb[l]))
        x = x + merged @ w_out[l]
    return rmsnorm(x, final_norm_w)
```

```python
import functools
import math

import jax
import jax.numpy as jnp
from jax import lax
from jax.experimental import pallas as pl
from jax.experimental.pallas import tpu as pltpu

F32 = jnp.float32
BF16 = jnp.bfloat16

D_MODEL = 1024
HEADS = 8
HEAD_W = 128
CONV_WIDTH = 5
DIFF_DH = 64
ROPE_DIM = DIFF_DH // 4
ROPE_THETA = 500000.0
EPS = 1e-6
GDN_CHUNK = 128
LANES = 128
VMEM_LIMIT = 56 * 1024 * 1024

COL_QA, COL_KA, COL_VA, COL_ZA = 0, 8, 16, 24
COL_QB, COL_KB, COL_ZB = 32, 40, 48
COL_GA, COL_GB = 56, 64
N_F32 = 72 * LANES
WIDE = HEADS * LANES


def _sigmoid(x):
    return 1.0 / (1.0 + jnp.exp(-x))


def _silu(x):
    return x * _sigmoid(x)


def _inproj_kernel(x_ref, nw_ref, w_ref, wg_ref, o_ref, vb_ref, g_ref, h_ref):
    j = pl.program_id(1)

    @pl.when(j == 0)
    def _():
        x = x_ref[...]
        ms = jnp.mean(x * x, axis=-1, keepdims=True)
        hb = (x * lax.rsqrt(ms + EPS) * nw_ref[...]).astype(BF16)
        h_ref[...] = hb
        g_ref[...] = jnp.dot(hb, wg_ref[...], preferred_element_type=F32)

    acc = jnp.dot(h_ref[...], w_ref[...], preferred_element_type=F32)
    last = pl.num_programs(1) - 1

    @pl.when(j < last)
    def _():
        o_ref[...] = acc

    @pl.when(j == last)
    def _():
        vb_ref[...] = acc.astype(BF16)


def _inproj(x2, norm_w, w_main, w_gate, tm):
    m = x2.shape[0]
    n_f32_blocks = N_F32 // WIDE
    return pl.pallas_call(
        _inproj_kernel,
        out_shape=(jax.ShapeDtypeStruct((m, N_F32), F32),
                   jax.ShapeDtypeStruct((m, WIDE), BF16),
                   jax.ShapeDtypeStruct((m, WIDE), F32)),
        grid=(m // tm, n_f32_blocks + 1),
        in_specs=[
            pl.BlockSpec((tm, D_MODEL), lambda i, j: (i, 0)),
            pl.BlockSpec((1, D_MODEL), lambda i, j: (0, 0)),
            pl.BlockSpec((D_MODEL, WIDE), lambda i, j: (0, j)),
            pl.BlockSpec((D_MODEL, WIDE), lambda i, j: (0, 0)),
        ],
        out_specs=(pl.BlockSpec((tm, WIDE), lambda i, j: (i, jnp.minimum(j, n_f32_blocks - 1))),
                   pl.BlockSpec((tm, WIDE), lambda i, j: (i, 0)),
                   pl.BlockSpec((tm, WIDE), lambda i, j: (i, 0))),
        scratch_shapes=[pltpu.VMEM((tm, D_MODEL), BF16)],
        compiler_params=pltpu.CompilerParams(
            dimension_semantics=("parallel", "arbitrary"), vmem_limit_bytes=VMEM_LIMIT),
        name="inproj",
    )(x2, norm_w, w_main, w_gate)


def _rope_kernel(q_ref, k_ref, c_ref, sa_ref, sb_ref, qo_ref, ko_ref):
    c = c_ref[0]
    sa = sa_ref[0]
    sb = sb_ref[0]
    for h in range(HEADS):
        sl = slice(h * LANES, (h + 1) * LANES)
        for src, dst, scale in ((q_ref, qo_ref, DIFF_DH ** -0.5), (k_ref, ko_ref, None)):
            t = src[0, :, sl]
            up = pltpu.roll(t, LANES - ROPE_DIM // 2, axis=1)
            dn = pltpu.roll(t, ROPE_DIM // 2, axis=1)
            r = t * c + up * sa + dn * sb
            if scale is not None:
                r = r * scale
            dst[0, :, sl] = r.astype(BF16)


def _rope(proj3, cos_t, sina_t, sinb_t, ts):
    b, s, _ = proj3.shape
    tab = pl.BlockSpec((1, ts, LANES), lambda i, j: (i, j, 0))
    return pl.pallas_call(
        _rope_kernel,
        out_shape=(jax.ShapeDtypeStruct((b, s, WIDE), BF16),
                   jax.ShapeDtypeStruct((b, s, WIDE), BF16)),
        grid=(b, s // ts),
        in_specs=[
            pl.BlockSpec((1, ts, WIDE), lambda i, j: (i, j, COL_QB // HEADS)),
            pl.BlockSpec((1, ts, WIDE), lambda i, j: (i, j, COL_KB // HEADS)),
            tab, tab, tab,
        ],
        out_specs=(pl.BlockSpec((1, ts, WIDE), lambda i, j: (i, j, 0)),
                   pl.BlockSpec((1, ts, WIDE), lambda i, j: (i, j, 0))),
        compiler_params=pltpu.CompilerParams(
            dimension_semantics=("parallel", "parallel"), vmem_limit_bytes=VMEM_LIMIT),
        name="rope",
    )(proj3, proj3, cos_t, sina_t, sinb_t)


def _attn_kernel(lam_ref, q_ref, k_ref, v_ref, z_ref, sw_ref, o_ref, ka_ref, kb_ref, vt_ref,
                 *, lambda_init):
    @pl.when(pl.program_id(2) == 0)
    def _():
        k = k_ref[0]
        lane = lax.broadcasted_iota(jnp.int32, k.shape, 1)
        zero = jnp.zeros_like(k)
        ka_ref[...] = jnp.where(lane < DIFF_DH, k, zero)
        kb_ref[...] = jnp.where(lane >= DIFF_DH, k, zero)
        vt_ref[...] = v_ref[0].astype(F32).T.astype(BF16)

    q = q_ref[0]
    lp = lam_ref[...]
    lam = (jnp.exp(jnp.sum(lp[0:1] * lp[1:2], axis=1, keepdims=True))
           - jnp.exp(jnp.sum(lp[2:3] * lp[3:4], axis=1, keepdims=True)) + lambda_init)

    def probs(kx_ref):
        st = lax.dot_general(kx_ref[...], q, (((1,), (1,)), ((), ())),
                             preferred_element_type=F32)
        e = jnp.exp(st - jnp.max(st, axis=0, keepdims=True))
        return e * (1.0 / jnp.sum(e, axis=0, keepdims=True))

    a = (probs(ka_ref) - lam * probs(kb_ref)).astype(BF16)
    ot = jnp.dot(vt_ref[...], a, preferred_element_type=F32)
    ms = jnp.mean(ot * ot, axis=0, keepdims=True)
    on = (ot * lax.rsqrt(ms + EPS)).T
    y = on * sw_ref[...] * (1.0 - lambda_init)
    o_ref[0] = (y * _silu(z_ref[0])).astype(BF16)


def _attention(q_r, k_r, vb3, proj3, lam_p, subln_w, lambda_init, tq):
    b, s, _ = proj3.shape
    kern = functools.partial(_attn_kernel, lambda_init=lambda_init)
    return pl.pallas_call(
        kern,
        out_shape=jax.ShapeDtypeStruct((b, s, WIDE), BF16),
        grid=(b, HEADS, s // tq),
        in_specs=[
            pl.BlockSpec((4, DIFF_DH), lambda i, h, j: (0, 0)),
            pl.BlockSpec((1, tq, LANES), lambda i, h, j: (i, j, h)),
            pl.BlockSpec((1, s, LANES), lambda i, h, j: (i, 0, h)),
            pl.BlockSpec((1, s, LANES), lambda i, h, j: (i, 0, h)),
            pl.BlockSpec((1, tq, LANES), lambda i, h, j: (i, j, COL_ZB + h)),
            pl.BlockSpec((1, LANES), lambda i, h, j: (0, 0)),
        ],
        out_specs=pl.BlockSpec((1, tq, LANES), lambda i, h, j: (i, j, h)),
        scratch_shapes=[pltpu.VMEM((s, LANES), BF16), pltpu.VMEM((s, LANES), BF16),
                        pltpu.VMEM((LANES, s), BF16)],
        compiler_params=pltpu.CompilerParams(
            dimension_semantics=("parallel", "parallel", "arbitrary"),
            vmem_limit_bytes=VMEM_LIMIT),
        name="diff_attn",
    )(lam_p, q_r, k_r, vb3, proj3, subln_w)


CONV_ROWS = 256
GATE_ROWS = 256
HALO = 8


def _gdn_kernel(alog_ref, dtb_ref, q_ref, k_ref, v_ref, z_ref, g_ref, cwq_ref, cwk_ref, cwv_ref,
                nw_ref, o_ref,
                qn_ref, kn_ref, vn_ref, gb_ref, e1_ref, e2_ref, e3_ref, grow_ref,
                u_ref, wq_ref, a2_ref, bd_ref):
    s = q_ref.shape[1]
    c = GDN_CHUNK
    nc = s // c
    head = pl.program_id(1)

    lane = lax.broadcasted_iota(jnp.int32, (1, LANES), 1)
    a_vec = jnp.where(lane == 2, alog_ref[0, head], alog_ref[1, head])
    dt_vec = jnp.where(lane == 2, dtb_ref[0, head], dtb_ref[1, head])
    neg_a = -jnp.exp(a_vec)
    gr = GATE_ROWS
    pos = lax.broadcasted_iota(jnp.int32, (gr, LANES), 0) & (c - 1)
    lane_full = lax.broadcasted_iota(jnp.int32, (gr, LANES), 1)

    def gate_block(i, carry):
        rows = pl.ds(pl.multiple_of(i * gr, gr), gr)
        x = g_ref[0, rows, :]
        xa = x + dt_vec
        softplus = jnp.maximum(xa, 0.0) + jnp.log(1.0 + jnp.exp(-jnp.abs(xa)))
        g = neg_a * softplus
        beta = _sigmoid(x)
        pre = g
        suf = g
        sh = 1
        while sh < c:
            pre = pre + jnp.where(pos >= sh, pltpu.roll(pre, sh, axis=0), 0.0)
            suf = suf + jnp.where(pos < c - sh, pltpu.roll(suf, gr - sh, axis=0), 0.0)
            sh *= 2
        is_fwd = lane_full == 2
        gc = jnp.where(is_fwd, pre, suf)
        rest = jnp.where(is_fwd, suf, pre) - g
        gb_ref[rows, :] = jnp.where(lane_full < 2, beta, gc)
        e1_ref[rows, :] = jnp.exp(gc)
        e2_ref[rows, :] = jnp.exp(rest)
        e3 = jnp.exp(gc + rest)
        for k in range(gr // c):
            ch8 = pl.multiple_of((i * (gr // c) + k) * 8, 8)
            e3_ref[pl.ds(ch8, 8), :] = e3[k * c:k * c + 8, :]
        grow_ref[:, rows] = gc.T[0:8, :]
        return carry

    lax.fori_loop(0, s // gr, gate_block, 0)

    n_conv = s // CONV_ROWS
    pad = CONV_WIDTH // 2
    for src, cw_ref, dst, mode in ((q_ref, cwq_ref, qn_ref, "q"), (k_ref, cwk_ref, kn_ref, "k"),
                                   (v_ref, cwv_ref, vn_ref, "v")):
        cw = cw_ref[...]

        def conv_block(i, carry, src=src, cw=cw, dst=dst, mode=mode):
            r0 = pl.multiple_of(i * CONV_ROWS, CONV_ROWS)
            lo = pl.multiple_of(jnp.maximum(r0 - HALO, 0), HALO)
            hi = pl.multiple_of(jnp.minimum(r0 + CONV_ROWS, s - HALO), HALO)
            before = jnp.where(i > 0, src[0, pl.ds(lo, HALO), :], 0.0)
            after = jnp.where(i < n_conv - 1, src[0, pl.ds(hi, HALO), :], 0.0)
            xb = jnp.concatenate([before, src[0, pl.ds(r0, CONV_ROWS), :], after], axis=0)
            y = jnp.zeros((CONV_ROWS, LANES), F32)
            for j in range(CONV_WIDTH):
                off = HALO - pad + j
                y = y + xb[off:off + CONV_ROWS, :] * cw[j:j + 1, :]
            y = _silu(y)
            if mode != "v":
                y = y * lax.rsqrt(jnp.sum(y * y, axis=-1, keepdims=True) + EPS)
            if mode == "q":
                y = y * (HEAD_W ** -0.5)
            dst[pl.ds(r0, CONV_ROWS), :] = y
            return carry

        lax.fori_loop(0, n_conv, conv_block, 0)

    ri = lax.broadcasted_iota(jnp.int32, (c, c), 0)
    ci = lax.broadcasted_iota(jnp.int32, (c, c), 1)
    eye = (ri == ci).astype(F32)
    masks = ((ri >= ci, ri > ci), (ri <= ci, ri < ci))
    bd_ref[...] = jnp.zeros((2 * c, 2 * c), BF16)

    def chunk_body(ch, carry):
        rows = pl.ds(pl.multiple_of(ch * c, c), c)
        kk = kn_ref[rows, :]
        qq = qn_ref[rows, :]
        vv = vn_ref[rows, :]
        kb16 = kk.astype(BF16)
        gram = lax.dot_general(kb16, kb16, (((1,), (1,)), ((), ())), preferred_element_type=F32)
        qk = lax.dot_general(qq.astype(BF16), kb16, (((1,), (1,)), ((), ())),
                             preferred_element_type=F32)
        gbv = gb_ref[rows, :]
        e1v = e1_ref[rows, :]
        e2v = e2_ref[rows, :]
        growv = grow_ref[:, rows]
        ls = []
        for d in (0, 1):
            incl, strict = masks[d]
            gd = gbv[:, 2 + d:3 + d] - growv[2 + d:3 + d, :]
            dec = jnp.exp(jnp.where(incl, gd, 0.0))
            ls.append(jnp.where(strict, gbv[:, d:d + 1] * gram * dec, 0.0))
            qkm = jnp.where(incl, qk * dec, 0.0)
            a2_ref[d, ch, 0:c, :] = qkm.astype(BF16)
            a2_ref[d, ch, c:2 * c, :] = (kk * e2v[:, 2 + d:3 + d]).T.astype(BF16)
            wq_ref[d, ch, c:2 * c, :] = (qq * e1v[:, 2 + d:3 + d]).astype(BF16)

        def set_bd(xf, xb):
            bd_ref[0:c, 0:c] = xf.astype(BF16)
            bd_ref[c:2 * c, c:2 * c] = xb.astype(BF16)

        lcat = jnp.concatenate(ls, axis=1)
        eye2 = jnp.concatenate([eye, eye], axis=1)
        pcat = eye2 - lcat
        set_bd(ls[0], ls[1])
        xcat = jnp.dot(lcat.astype(BF16), bd_ref[...], preferred_element_type=F32)
        power = 2
        while power < c:
            set_bd(xcat[:, 0:c], xcat[:, c:2 * c])
            if 2 * power < c:
                both = jnp.concatenate([pcat, xcat], axis=0).astype(BF16)
                res = jnp.dot(both, bd_ref[...], preferred_element_type=F32)
                pcat = pcat + res[0:c, :]
                xcat = res[c:2 * c, :]
            else:
                pcat = pcat + jnp.dot(pcat.astype(BF16), bd_ref[...], preferred_element_type=F32)
            power *= 2

        ncat = pcat - eye2
        set_bd(ncat[:, 0:c], ncat[:, c:2 * c])
        res = -(ncat + lcat + jnp.dot(lcat.astype(BF16), bd_ref[...], preferred_element_type=F32))
        set_bd(res[:, 0:c], res[:, c:2 * c])
        pcat = pcat + res + jnp.dot(ncat.astype(BF16), bd_ref[...], preferred_element_type=F32)

        for d in (0, 1):
            t16 = pcat[:, d * c:(d + 1) * c].astype(BF16)
            bcol = gbv[:, d:d + 1]
            rhs = jnp.concatenate([vv * bcol, kk * (bcol * e1v[:, 2 + d:3 + d])], axis=1)
            uw = jnp.dot(t16, rhs.astype(BF16), preferred_element_type=F32)
            u_ref[d, rows, :] = uw[:, 0:LANES]
            wq_ref[d, ch, 0:c, :] = uw[:, LANES:2 * LANES].astype(BF16)
        return carry

    lax.fori_loop(0, nc, chunk_body, 0)

    of_ref, ob_ref = qn_ref, kn_ref

    def scan_body(t, carry):
        new_states = []
        for d, out_ref in ((0, of_ref), (1, ob_ref)):
            ch = t if d == 0 else nc - 1 - t
            rows = pl.ds(pl.multiple_of(ch * c, c), c)
            st = carry[d]
            r1 = jnp.dot(wq_ref[d, ch], st.astype(BF16), preferred_element_type=F32)
            vnew = u_ref[d, rows, :] - r1[0:c, :]
            r2 = jnp.dot(a2_ref[d, ch], vnew.astype(BF16), preferred_element_type=F32)
            out_ref[rows, :] = r1[c:2 * c, :] + r2[0:c, :]
            decay = e3_ref[pl.ds(pl.multiple_of(ch * 8, 8), 8), :][0:1, 2 + d:3 + d]
            new_states.append(st * decay + r2[c:2 * c, :])
        return tuple(new_states)

    zero_state = jnp.zeros((HEAD_W, HEAD_W), F32)
    lax.fori_loop(0, nc, scan_body, (zero_state, zero_state))

    def out_block(i, carry):
        rows = pl.ds(pl.multiple_of(i * gr, gr), gr)
        o = of_ref[rows, :] + ob_ref[rows, :]
        ms = jnp.mean(o * o, axis=-1, keepdims=True)
        y = o * lax.rsqrt(ms + EPS) * nw_ref[...]
        o_ref[0, rows, :] = (y * _silu(z_ref[0, rows, :])).astype(BF16)
        return carry

    lax.fori_loop(0, s // gr, out_block, 0)


def _gdn(proj3, gates3, conv_w, a_log, dt_bias, norm_w):
    b, s, _ = proj3.shape
    c = GDN_CHUNK
    nc = s // c
    col = lambda off: pl.BlockSpec((1, s, LANES), lambda i, h: (i, 0, off + h))
    cw = lambda off: pl.BlockSpec((CONV_WIDTH, LANES), lambda i, h: (0, off + h))
    smem = pl.BlockSpec(memory_space=pltpu.SMEM)
    return pl.pallas_call(
        _gdn_kernel,
        out_shape=jax.ShapeDtypeStruct((b, s, WIDE), BF16),
        grid=(b, HEADS),
        in_specs=[
            smem, smem,
            col(COL_QA), col(COL_KA), col(COL_VA), col(COL_ZA),
            pl.BlockSpec((1, s, LANES), lambda i, h: (i, 0, h)),
            cw(0), cw(HEADS), cw(2 * HEADS),
            pl.BlockSpec((1, LANES), lambda i, h: (0, 0)),
        ],
        out_specs=pl.BlockSpec((1, s, LANES), lambda i, h: (i, 0, h)),
        scratch_shapes=[
            pltpu.VMEM((s, LANES), F32),
            pltpu.VMEM((s, LANES), F32),
            pltpu.VMEM((s, LANES), F32),
            pltpu.VMEM((s, LANES), F32),
            pltpu.VMEM((s, LANES), F32),
            pltpu.VMEM((s, LANES), F32),
            pltpu.VMEM((nc * 8, LANES), F32),
            pltpu.VMEM((8, s), F32),
            pltpu.VMEM((2, s, LANES), F32),
            pltpu.VMEM((2, nc, 2 * c, LANES), BF16),
            pltpu.VMEM((2, nc, 2 * c, c), BF16),
            pltpu.VMEM((2 * c, 2 * c), BF16),
        ],
        compiler_params=pltpu.CompilerParams(
            dimension_semantics=("parallel", "parallel"), vmem_limit_bytes=VMEM_LIMIT),
        name="gdn",
    )(a_log, dt_bias, proj3, proj3, proj3, proj3, gates3, conv_w, conv_w, conv_w, norm_w)


def _outproj_kernel(x_ref, ya_ref, yb_ref, ga_ref, gb_ref, wpa_ref, wpb_ref, wo_ref, fw_ref, o_ref,
                    *, final_norm):
    ta = jnp.dot(ya_ref[...], wpa_ref[...], preferred_element_type=F32)
    tb = jnp.dot(yb_ref[...], wpb_ref[...], preferred_element_type=F32)
    merged = _sigmoid(ga_ref[...]) * ta + _sigmoid(gb_ref[...]) * tb
    xn = x_ref[...] + jnp.dot(merged.astype(BF16), wo_ref[...], preferred_element_type=F32)
    if final_norm:
        ms = jnp.mean(xn * xn, axis=-1, keepdims=True)
        xn = xn * lax.rsqrt(ms + EPS) * fw_ref[...]
    o_ref[...] = xn


def _outproj(x2, ya2, yb2, proj2, w_pa, w_pb, w_out, final_w, final_norm, tm):
    m = x2.shape[0]
    row = lambda blk: pl.BlockSpec((tm, D_MODEL), lambda i: (i, blk))
    wfull = pl.BlockSpec((D_MODEL, D_MODEL), lambda i: (0, 0))
    kern = functools.partial(_outproj_kernel, final_norm=final_norm)
    return pl.pallas_call(
        kern,
        out_shape=jax.ShapeDtypeStruct((m, D_MODEL), F32),
        grid=(m // tm,),
        in_specs=[row(0), row(0), row(0), row(COL_GA // HEADS), row(COL_GB // HEADS),
                  wfull, wfull, wfull, pl.BlockSpec((1, D_MODEL), lambda i: (0, 0))],
        out_specs=row(0),
        compiler_params=pltpu.CompilerParams(
            dimension_semantics=("parallel",), vmem_limit_bytes=VMEM_LIMIT),
        name="outproj",
    )(x2, ya2, yb2, proj2, proj2, w_pa, w_pb, w_out, final_w)


def _pick(n, prefs):
    for p in prefs:
        if n % p == 0:
            return p
    return n


def _rope_tables(positions):
    inv_freq = ROPE_THETA ** (-(jnp.arange(0, ROPE_DIM, 2, dtype=F32) / ROPE_DIM))
    ang = positions.astype(F32)[..., None] * inv_freq
    cos, sin = jnp.cos(ang), jnp.sin(ang)
    half = ROPE_DIM // 2
    b, s, _ = ang.shape
    ones = jnp.ones((b, s, DIFF_DH - ROPE_DIM), F32)
    zeros8 = jnp.zeros((b, s, half), F32)
    zrest = jnp.zeros((b, s, DIFF_DH - ROPE_DIM), F32)
    cos64 = jnp.concatenate([cos, cos, ones], axis=-1)
    sina64 = jnp.concatenate([-sin, zeros8, zrest], axis=-1)
    sinb64 = jnp.concatenate([zeros8, sin, zrest], axis=-1)
    dup = lambda t: jnp.concatenate([t, t], axis=-1)
    return dup(cos64), dup(sina64), dup(sinb64)


def _split_w_in(w_in):
    depth = w_in.shape[0]
    seg = lambda k: w_in[:, :, k * WIDE:(k + 1) * WIDE]
    g0 = 3 * WIDE
    seg2 = lambda k: w_in[:, :, g0 + 4 * HEADS + k * WIDE:g0 + 4 * HEADS + (k + 1) * WIDE]
    w_main = jnp.concatenate([seg(0), seg(1), seg(2), seg2(0), seg2(1), seg2(2), seg2(4), seg2(5),
                              seg2(6), seg2(3)], axis=-1).astype(BF16)
    wg = w_in[:, :, g0:g0 + 4 * HEADS].reshape(depth, D_MODEL, 4, HEADS).transpose(0, 1, 3, 2)
    wg = jnp.pad(wg, ((0, 0), (0, 0), (0, 0), (0, LANES - 4))).reshape(depth, D_MODEL, WIDE)
    return w_main, wg.astype(BF16)


def kernel(x, positions, norm_w, w_in, conv_w, a_log, dt_bias, gdn_norm_w, diff_lambda,
           diff_subln_w, w_pa, w_pb, w_out, final_norm_w):
    b, s, d = x.shape
    depth = w_in.shape[0]
    m = b * s
    assert d == D_MODEL and s % GDN_CHUNK == 0 and s % CONV_ROWS == 0 and s % GATE_ROWS == 0
    tm_in = _pick(m, (1024, 512, 256))
    tm_out = _pick(m, (512, 256))
    ts_rope = _pick(s, (512, 256))
    tq = _pick(s, (256, 128))

    cos_t, sina_t, sinb_t = _rope_tables(positions)
    w_main, w_gate = _split_w_in(w_in)
    w_pa16, w_pb16, w_out16 = w_pa.astype(BF16), w_pb.astype(BF16), w_out.astype(BF16)

    x2 = x.reshape(m, d)
    for l in range(depth):
        proj2, vb2, gates2 = _inproj(x2, norm_w[l][None], w_main[l], w_gate[l], tm_in)
        proj3 = proj2.reshape(b, s, N_F32)
        y_a = _gdn(proj3, gates2.reshape(b, s, WIDE), conv_w[l], a_log[l], dt_bias[l],
                   gdn_norm_w[l][None])
        q_r, k_r = _rope(proj3, cos_t, sina_t, sinb_t, ts_rope)
        lambda_init = 0.8 - 0.6 * math.exp(-0.3 * l)
        y_b = _attention(q_r, k_r, vb2.reshape(b, s, WIDE), proj3, diff_lambda[l],
                         diff_subln_w[l][None], lambda_init, tq)
        x2 = _outproj(x2, y_a.reshape(m, d), y_b.reshape(m, d), proj2, w_pa16[l], w_pb16[l],
                      w_out16[l], final_norm_w[None], l == depth - 1, tm_out)
    return x2.reshape(b, s, d)
```

```python
import functools
import math

import jax
import jax.numpy as jnp
from jax import lax
from jax.experimental import pallas as pl
from jax.experimental.pallas import tpu as pltpu

F32 = jnp.float32
BF16 = jnp.bfloat16

D_MODEL = 1024
HEADS = 8
HEAD_W = 128
CONV_WIDTH = 5
DIFF_DH = 64
ROPE_DIM = DIFF_DH // 4
ROPE_THETA = 500000.0
EPS = 1e-6
GDN_CHUNK = 128
LANES = 128
VMEM_LIMIT = 56 * 1024 * 1024

COL_QA, COL_KA, COL_VA, COL_ZA = 0, 8, 16, 24
COL_QB, COL_KB, COL_ZB = 32, 40, 48
COL_GA, COL_GB = 56, 64
N_F32 = 72 * LANES
WIDE = HEADS * LANES


def _sigmoid(x):
    return 1.0 / (1.0 + jnp.exp(-x))


def _silu(x):
    return x * _sigmoid(x)


def _inproj_kernel(x_ref, nw_ref, w_ref, wg_ref, o_ref, vb_ref, g_ref, h_ref):
    j = pl.program_id(1)

    @pl.when(j == 0)
    def _():
        x = x_ref[...]
        ms = jnp.mean(x * x, axis=-1, keepdims=True)
        hb = (x * lax.rsqrt(ms + EPS) * nw_ref[...]).astype(BF16)
        h_ref[...] = hb
        g_ref[...] = jnp.dot(hb, wg_ref[...], preferred_element_type=F32)

    acc = jnp.dot(h_ref[...], w_ref[...], preferred_element_type=F32)
    last = pl.num_programs(1) - 1

    @pl.when(j < last)
    def _():
        o_ref[...] = acc

    @pl.when(j == last)
    def _():
        vb_ref[...] = acc.astype(BF16)


def _inproj(x2, norm_w, w_main, w_gate, tm):
    m = x2.shape[0]
    n_f32_blocks = N_F32 // WIDE
    return pl.pallas_call(
        _inproj_kernel,
        out_shape=(jax.ShapeDtypeStruct((m, N_F32), F32),
                   jax.ShapeDtypeStruct((m, WIDE), BF16),
                   jax.ShapeDtypeStruct((m, WIDE), F32)),
        grid=(m // tm, n_f32_blocks + 1),
        in_specs=[
            pl.BlockSpec((tm, D_MODEL), lambda i, j: (i, 0)),
            pl.BlockSpec((1, D_MODEL), lambda i, j: (0, 0)),
            pl.BlockSpec((D_MODEL, WIDE), lambda i, j: (0, j)),
            pl.BlockSpec((D_MODEL, WIDE), lambda i, j: (0, 0)),
        ],
        out_specs=(pl.BlockSpec((tm, WIDE), lambda i, j: (i, jnp.minimum(j, n_f32_blocks - 1))),
                   pl.BlockSpec((tm, WIDE), lambda i, j: (i, 0)),
                   pl.BlockSpec((tm, WIDE), lambda i, j: (i, 0))),
        scratch_shapes=[pltpu.VMEM((tm, D_MODEL), BF16)],
        compiler_params=pltpu.CompilerParams(
            dimension_semantics=("parallel", "arbitrary"), vmem_limit_bytes=VMEM_LIMIT),
        name="inproj",
    )(x2, norm_w, w_main, w_gate)


def _rope_kernel(q_ref, k_ref, c_ref, sa_ref, sb_ref, qo_ref, ko_ref):
    c = c_ref[0]
    sa = sa_ref[0]
    sb = sb_ref[0]
    for h in range(HEADS):
        sl = slice(h * LANES, (h + 1) * LANES)
        for src, dst, scale in ((q_ref, qo_ref, DIFF_DH ** -0.5), (k_ref, ko_ref, None)):
            t = src[0, :, sl]
            up = pltpu.roll(t, LANES - ROPE_DIM // 2, axis=1)
            dn = pltpu.roll(t, ROPE_DIM // 2, axis=1)
            r = t * c + up * sa + dn * sb
            if scale is not None:
                r = r * scale
            dst[0, :, sl] = r.astype(BF16)


def _rope(proj3, cos_t, sina_t, sinb_t, ts):
    b, s, _ = proj3.shape
    tab = pl.BlockSpec((1, ts, LANES), lambda i, j: (i, j, 0))
    return pl.pallas_call(
        _rope_kernel,
        out_shape=(jax.ShapeDtypeStruct((b, s, WIDE), BF16),
                   jax.ShapeDtypeStruct((b, s, WIDE), BF16)),
        grid=(b, s // ts),
        in_specs=[
            pl.BlockSpec((1, ts, WIDE), lambda i, j: (i, j, COL_QB // HEADS)),
            pl.BlockSpec((1, ts, WIDE), lambda i, j: (i, j, COL_KB // HEADS)),
            tab, tab, tab,
        ],
        out_specs=(pl.BlockSpec((1, ts, WIDE), lambda i, j: (i, j, 0)),
                   pl.BlockSpec((1, ts, WIDE), lambda i, j: (i, j, 0))),
        compiler_params=pltpu.CompilerParams(
            dimension_semantics=("parallel", "parallel"), vmem_limit_bytes=VMEM_LIMIT),
        name="rope",
    )(proj3, proj3, cos_t, sina_t, sinb_t)


def _attn_kernel(lam_ref, q_ref, k_ref, v_ref, z_ref, sw_ref, o_ref, ka_ref, kb_ref, vt_ref,
                 *, lambda_init):
    @pl.when(pl.program_id(2) == 0)
    def _():
        k = k_ref[0]
        lane = lax.broadcasted_iota(jnp.int32, k.shape, 1)
        zero = jnp.zeros_like(k)
        ka_ref[...] = jnp.where(lane < DIFF_DH, k, zero)
        kb_ref[...] = jnp.where(lane >= DIFF_DH, k, zero)
        vt_ref[...] = v_ref[0].astype(F32).T.astype(BF16)

    q = q_ref[0]
    lp = lam_ref[...]
    lam = (jnp.exp(jnp.sum(lp[0:1] * lp[1:2], axis=1, keepdims=True))
           - jnp.exp(jnp.sum(lp[2:3] * lp[3:4], axis=1, keepdims=True)) + lambda_init)

    def probs(kx_ref):
        st = lax.dot_general(kx_ref[...], q, (((1,), (1,)), ((), ())),
                             preferred_element_type=F32)
        e = jnp.exp(st - jnp.max(st, axis=0, keepdims=True))
        return e * (1.0 / jnp.sum(e, axis=0, keepdims=True))

    a = (probs(ka_ref) - lam * probs(kb_ref)).astype(BF16)
    ot = jnp.dot(vt_ref[...], a, preferred_element_type=F32)
    ms = jnp.mean(ot * ot, axis=0, keepdims=True)
    on = (ot * lax.rsqrt(ms + EPS)).T
    y = on * sw_ref[...] * (1.0 - lambda_init)
    o_ref[0] = (y * _silu(z_ref[0])).astype(BF16)


def _attention(q_r, k_r, vb3, proj3, lam_p, subln_w, lambda_init, tq):
    b, s, _ = proj3.shape
    kern = functools.partial(_attn_kernel, lambda_init=lambda_init)
    return pl.pallas_call(
        kern,
        out_shape=jax.ShapeDtypeStruct((b, s, WIDE), BF16),
        grid=(b, HEADS, s // tq),
        in_specs=[
            pl.BlockSpec((4, DIFF_DH), lambda i, h, j: (0, 0)),
            pl.BlockSpec((1, tq, LANES), lambda i, h, j: (i, j, h)),
            pl.BlockSpec((1, s, LANES), lambda i, h, j: (i, 0, h)),
            pl.BlockSpec((1, s, LANES), lambda i, h, j: (i, 0, h)),
            pl.BlockSpec((1, tq, LANES), lambda i, h, j: (i, j, COL_ZB + h)),
            pl.BlockSpec((1, LANES), lambda i, h, j: (0, 0)),
        ],
        out_specs=pl.BlockSpec((1, tq, LANES), lambda i, h, j: (i, j, h)),
        scratch_shapes=[pltpu.VMEM((s, LANES), BF16), pltpu.VMEM((s, LANES), BF16),
                        pltpu.VMEM((LANES, s), BF16)],
        compiler_params=pltpu.CompilerParams(
            dimension_semantics=("parallel", "parallel", "arbitrary"),
            vmem_limit_bytes=VMEM_LIMIT),
        name="diff_attn",
    )(lam_p, q_r, k_r, vb3, proj3, subln_w)


CONV_ROWS = 256
GATE_ROWS = 256
HALO = 8


def _gdn_kernel(alog_ref, dtb_ref, q_ref, k_ref, v_ref, z_ref, g_ref, cwq_ref, cwk_ref, cwv_ref,
                nw_ref, o_ref,
                qn_ref, kn_ref, vn_ref, gb_ref, e1_ref, e2_ref, e3_ref, grow_ref,
                u_ref, wq_ref, a2_ref):
    s = q_ref.shape[1]
    c = GDN_CHUNK
    nc = s // c
    head = pl.program_id(1)

    lane = lax.broadcasted_iota(jnp.int32, (1, LANES), 1)
    a_vec = jnp.where(lane == 2, alog_ref[0, head], alog_ref[1, head])
    dt_vec = jnp.where(lane == 2, dtb_ref[0, head], dtb_ref[1, head])
    neg_a = -jnp.exp(a_vec)
    gr = GATE_ROWS
    pos = lax.broadcasted_iota(jnp.int32, (gr, LANES), 0) & (c - 1)
    lane_full = lax.broadcasted_iota(jnp.int32, (gr, LANES), 1)

    def gate_block(i, carry):
        rows = pl.ds(pl.multiple_of(i * gr, gr), gr)
        x = g_ref[0, rows, :]
        xa = x + dt_vec
        softplus = jnp.maximum(xa, 0.0) + jnp.log(1.0 + jnp.exp(-jnp.abs(xa)))
        g = neg_a * softplus
        beta = _sigmoid(x)
        pre = g
        suf = g
        sh = 1
        while sh < c:
            pre = pre + jnp.where(pos >= sh, pltpu.roll(pre, sh, axis=0), 0.0)
            suf = suf + jnp.where(pos < c - sh, pltpu.roll(suf, gr - sh, axis=0), 0.0)
            sh *= 2
        is_fwd = lane_full == 2
        gc = jnp.where(is_fwd, pre, suf)
        rest = jnp.where(is_fwd, suf, pre) - g
        gb_ref[rows, :] = jnp.where(lane_full < 2, beta, gc)
        e1_ref[rows, :] = jnp.exp(gc)
        e2_ref[rows, :] = jnp.exp(rest)
        e3 = jnp.exp(gc + rest)
        for k in range(gr // c):
            ch8 = pl.multiple_of((i * (gr // c) + k) * 8, 8)
            e3_ref[pl.ds(ch8, 8), :] = e3[k * c:k * c + 8, :]
        grow_ref[:, rows] = gc.T[0:8, :]
        return carry

    lax.fori_loop(0, s // gr, gate_block, 0)

    n_conv = s // CONV_ROWS
    pad = CONV_WIDTH // 2
    for src, cw_ref, dst, mode in ((q_ref, cwq_ref, qn_ref, "q"), (k_ref, cwk_ref, kn_ref, "k"),
                                   (v_ref, cwv_ref, vn_ref, "v")):
        cw = cw_ref[...]

        def conv_block(i, carry, src=src, cw=cw, dst=dst, mode=mode):
            r0 = pl.multiple_of(i * CONV_ROWS, CONV_ROWS)
            lo = pl.multiple_of(jnp.maximum(r0 - HALO, 0), HALO)
            hi = pl.multiple_of(jnp.minimum(r0 + CONV_ROWS, s - HALO), HALO)
            before = jnp.where(i > 0, src[0, pl.ds(lo, HALO), :], 0.0)
            after = jnp.where(i < n_conv - 1, src[0, pl.ds(hi, HALO), :], 0.0)
            xb = jnp.concatenate([before, src[0, pl.ds(r0, CONV_ROWS), :], after], axis=0)
            y = jnp.zeros((CONV_ROWS, LANES), F32)
            for j in range(CONV_WIDTH):
                off = HALO - pad + j
                y = y + xb[off:off + CONV_ROWS, :] * cw[j:j + 1, :]
            y = _silu(y)
            if mode != "v":
                y = y * lax.rsqrt(jnp.sum(y * y, axis=-1, keepdims=True) + EPS)
            if mode == "q":
                y = y * (HEAD_W ** -0.5)
            dst[pl.ds(r0, CONV_ROWS), :] = y
            return carry

        lax.fori_loop(0, n_conv, conv_block, 0)

    ri = lax.broadcasted_iota(jnp.int32, (c, c), 0)
    ci = lax.broadcasted_iota(jnp.int32, (c, c), 1)
    eye = (ri == ci).astype(F32)
    masks = ((ri >= ci, ri > ci), (ri <= ci, ri < ci))
    zblk = jnp.zeros((c, c), BF16)

    def block_diag(xcat):
        x16 = xcat.astype(BF16)
        return jnp.concatenate([jnp.concatenate([x16[:, 0:c], zblk], axis=1),
                                jnp.concatenate([zblk, x16[:, c:2 * c]], axis=1)], axis=0)

    def chunk_group(chs):
        rows = [pl.ds(pl.multiple_of(ch * c, c), c) for ch in chs]
        n = len(chs)
        grams, qks = [], []
        for r in rows:
            kb16 = kn_ref[r, :].astype(BF16)
            grams.append(lax.dot_general(kb16, kb16, (((1,), (1,)), ((), ())),
                                         preferred_element_type=F32))
            qks.append(lax.dot_general(qn_ref[r, :].astype(BF16), kb16, (((1,), (1,)), ((), ())),
                                       preferred_element_type=F32))
        lcats = []
        for ch, r, gram, qk in zip(chs, rows, grams, qks):
            kk = kn_ref[r, :]
            qq = qn_ref[r, :]
            gbv = gb_ref[r, :]
            e1v = e1_ref[r, :]
            e2v = e2_ref[r, :]
            growv = grow_ref[:, r]
            ls = []
            for d in (0, 1):
                incl, strict = masks[d]
                gd = gbv[:, 2 + d:3 + d] - growv[2 + d:3 + d, :]
                dec = jnp.exp(jnp.where(incl, gd, 0.0))
                ls.append(jnp.where(strict, gbv[:, d:d + 1] * gram * dec, 0.0))
                qkm = jnp.where(incl, qk * dec, 0.0)
                a2_ref[d, ch, 0:c, :] = qkm.astype(BF16)
                a2_ref[d, ch, c:2 * c, :] = (kk * e2v[:, 2 + d:3 + d]).T.astype(BF16)
                wq_ref[d, ch, c:2 * c, :] = (qq * e1v[:, 2 + d:3 + d]).astype(BF16)
            lcats.append(jnp.concatenate(ls, axis=1))

        eye2 = jnp.concatenate([eye, eye], axis=1)
        l16s = [l.astype(BF16) for l in lcats]
        pcats = [eye2 - l for l in lcats]
        xcats = [jnp.dot(l16, block_diag(l), preferred_element_type=F32)
                 for l16, l in zip(l16s, lcats)]
        power = 2
        while power < c:
            for i in range(n):
                bd = block_diag(xcats[i])
                if 2 * power < c:
                    both = jnp.concatenate([pcats[i], xcats[i]], axis=0).astype(BF16)
                    res = jnp.dot(both, bd, preferred_element_type=F32)
                    pcats[i] = pcats[i] + res[0:c, :]
                    xcats[i] = res[c:2 * c, :]
                else:
                    pcats[i] = pcats[i] + jnp.dot(pcats[i].astype(BF16), bd,
                                                  preferred_element_type=F32)
            power *= 2

        ncats = [p - eye2 for p in pcats]
        ress = [-(nc_ + l + jnp.dot(l16, block_diag(nc_), preferred_element_type=F32))
                for nc_, l, l16 in zip(ncats, lcats, l16s)]
        pcats = [p + r_ + jnp.dot(nc_.astype(BF16), block_diag(r_), preferred_element_type=F32)
                 for p, r_, nc_ in zip(pcats, ress, ncats)]

        for ch, r, pcat in zip(chs, rows, pcats):
            kk = kn_ref[r, :]
            vv = vn_ref[r, :]
            gbv = gb_ref[r, :]
            e1v = e1_ref[r, :]
            for d in (0, 1):
                t16 = pcat[:, d * c:(d + 1) * c].astype(BF16)
                bcol = gbv[:, d:d + 1]
                rhs = jnp.concatenate([vv * bcol, kk * (bcol * e1v[:, 2 + d:3 + d])], axis=1)
                uw = jnp.dot(t16, rhs.astype(BF16), preferred_element_type=F32)
                u_ref[d, r, :] = uw[:, 0:LANES]
                wq_ref[d, ch, 0:c, :] = uw[:, LANES:2 * LANES].astype(BF16)

    group = max(g for g in (4, 2, 1) if nc % g == 0)

    def chunk_body(i, carry):
        chunk_group([i * group + k for k in range(group)])
        return carry

    lax.fori_loop(0, nc // group, chunk_body, 0)

    of_ref, ob_ref = qn_ref, kn_ref

    def scan_body(t, carry):
        new_states = []
        for d, out_ref in ((0, of_ref), (1, ob_ref)):
            ch = t if d == 0 else nc - 1 - t
            rows = pl.ds(pl.multiple_of(ch * c, c), c)
            st = carry[d]
            r1 = jnp.dot(wq_ref[d, ch], st.astype(BF16), preferred_element_type=F32)
            vnew = u_ref[d, rows, :] - r1[0:c, :]
            r2 = jnp.dot(a2_ref[d, ch], vnew.astype(BF16), preferred_element_type=F32)
            out_ref[rows, :] = r1[c:2 * c, :] + r2[0:c, :]
            decay = e3_ref[pl.ds(pl.multiple_of(ch * 8, 8), 8), :][0:1, 2 + d:3 + d]
            new_states.append(st * decay + r2[c:2 * c, :])
        return tuple(new_states)

    zero_state = jnp.zeros((HEAD_W, HEAD_W), F32)
    lax.fori_loop(0, nc, scan_body, (zero_state, zero_state))

    def out_block(i, carry):
        rows = pl.ds(pl.multiple_of(i * gr, gr), gr)
        o = of_ref[rows, :] + ob_ref[rows, :]
        ms = jnp.mean(o * o, axis=-1, keepdims=True)
        y = o * lax.rsqrt(ms + EPS) * nw_ref[...]
        o_ref[0, rows, :] = (y * _silu(z_ref[0, rows, :])).astype(BF16)
        return carry

    lax.fori_loop(0, s // gr, out_block, 0)


def _gdn(proj3, gates3, conv_w, a_log, dt_bias, norm_w):
    b, s, _ = proj3.shape
    c = GDN_CHUNK
    nc = s // c
    col = lambda off: pl.BlockSpec((1, s, LANES), lambda i, h: (i, 0, off + h))
    cw = lambda off: pl.BlockSpec((CONV_WIDTH, LANES), lambda i, h: (0, off + h))
    smem = pl.BlockSpec(memory_space=pltpu.SMEM)
    return pl.pallas_call(
        _gdn_kernel,
        out_shape=jax.ShapeDtypeStruct((b, s, WIDE), BF16),
        grid=(b, HEADS),
        in_specs=[
            smem, smem,
            col(COL_QA), col(COL_KA), col(COL_VA), col(COL_ZA),
            pl.BlockSpec((1, s, LANES), lambda i, h: (i, 0, h)),
            cw(0), cw(HEADS), cw(2 * HEADS),
            pl.BlockSpec((1, LANES), lambda i, h: (0, 0)),
        ],
        out_specs=pl.BlockSpec((1, s, LANES), lambda i, h: (i, 0, h)),
        scratch_shapes=[
            pltpu.VMEM((s, LANES), F32),
            pltpu.VMEM((s, LANES), F32),
            pltpu.VMEM((s, LANES), F32),
            pltpu.VMEM((s, LANES), F32),
            pltpu.VMEM((s, LANES), F32),
            pltpu.VMEM((s, LANES), F32),
            pltpu.VMEM((nc * 8, LANES), F32),
            pltpu.VMEM((8, s), F32),
            pltpu.VMEM((2, s, LANES), F32),
            pltpu.VMEM((2, nc, 2 * c, LANES), BF16),
            pltpu.VMEM((2, nc, 2 * c, c), BF16),
        ],
        compiler_params=pltpu.CompilerParams(
            dimension_semantics=("parallel", "parallel"), vmem_limit_bytes=VMEM_LIMIT),
        name="gdn",
    )(a_log, dt_bias, proj3, proj3, proj3, proj3, gates3, conv_w, conv_w, conv_w, norm_w)


def _outproj_kernel(x_ref, ya_ref, yb_ref, ga_ref, gb_ref, wpa_ref, wpb_ref, wo_ref, fw_ref, o_ref,
                    *, final_norm):
    ta = jnp.dot(ya_ref[...], wpa_ref[...], preferred_element_type=F32)
    tb = jnp.dot(yb_ref[...], wpb_ref[...], preferred_element_type=F32)
    merged = _sigmoid(ga_ref[...]) * ta + _sigmoid(gb_ref[...]) * tb
    xn = x_ref[...] + jnp.dot(merged.astype(BF16), wo_ref[...], preferred_element_type=F32)
    if final_norm:
        ms = jnp.mean(xn * xn, axis=-1, keepdims=True)
        xn = xn * lax.rsqrt(ms + EPS) * fw_ref[...]
    o_ref[...] = xn


def _outproj(x2, ya2, yb2, proj2, w_pa, w_pb, w_out, final_w, final_norm, tm):
    m = x2.shape[0]
    row = lambda blk: pl.BlockSpec((tm, D_MODEL), lambda i: (i, blk))
    wfull = pl.BlockSpec((D_MODEL, D_MODEL), lambda i: (0, 0))
    kern = functools.partial(_outproj_kernel, final_norm=final_norm)
    return pl.pallas_call(
        kern,
        out_shape=jax.ShapeDtypeStruct((m, D_MODEL), F32),
        grid=(m // tm,),
        in_specs=[row(0), row(0), row(0), row(COL_GA // HEADS), row(COL_GB // HEADS),
                  wfull, wfull, wfull, pl.BlockSpec((1, D_MODEL), lambda i: (0, 0))],
        out_specs=row(0),
        compiler_params=pltpu.CompilerParams(
            dimension_semantics=("parallel",), vmem_limit_bytes=VMEM_LIMIT),
        name="outproj",
    )(x2, ya2, yb2, proj2, proj2, w_pa, w_pb, w_out, final_w)


def _pick(n, prefs):
    for p in prefs:
        if n % p == 0:
            return p
    return n


def _rope_tables(positions):
    inv_freq = ROPE_THETA ** (-(jnp.arange(0, ROPE_DIM, 2, dtype=F32) / ROPE_DIM))
    ang = positions.astype(F32)[..., None] * inv_freq
    cos, sin = jnp.cos(ang), jnp.sin(ang)
    half = ROPE_DIM // 2
    b, s, _ = ang.shape
    ones = jnp.ones((b, s, DIFF_DH - ROPE_DIM), F32)
    zeros8 = jnp.zeros((b, s, half), F32)
    zrest = jnp.zeros((b, s, DIFF_DH - ROPE_DIM), F32)
    cos64 = jnp.concatenate([cos, cos, ones], axis=-1)
    sina64 = jnp.concatenate([-sin, zeros8, zrest], axis=-1)
    sinb64 = jnp.concatenate([zeros8, sin, zrest], axis=-1)
    dup = lambda t: jnp.concatenate([t, t], axis=-1)
    return dup(cos64), dup(sina64), dup(sinb64)


def _split_w_in(w_in):
    depth = w_in.shape[0]
    seg = lambda k: w_in[:, :, k * WIDE:(k + 1) * WIDE]
    g0 = 3 * WIDE
    seg2 = lambda k: w_in[:, :, g0 + 4 * HEADS + k * WIDE:g0 + 4 * HEADS + (k + 1) * WIDE]
    w_main = jnp.concatenate([seg(0), seg(1), seg(2), seg2(0), seg2(1), seg2(2), seg2(4), seg2(5),
                              seg2(6), seg2(3)], axis=-1).astype(BF16)
    wg = w_in[:, :, g0:g0 + 4 * HEADS].reshape(depth, D_MODEL, 4, HEADS).transpose(0, 1, 3, 2)
    wg = jnp.pad(wg, ((0, 0), (0, 0), (0, 0), (0, LANES - 4))).reshape(depth, D_MODEL, WIDE)
    return w_main, wg.astype(BF16)


def kernel(x, positions, norm_w, w_in, conv_w, a_log, dt_bias, gdn_norm_w, diff_lambda,
           diff_subln_w, w_pa, w_pb, w_out, final_norm_w):
    b, s, d = x.shape
    depth = w_in.shape[0]
    m = b * s
    assert d == D_MODEL and s % GDN_CHUNK == 0 and s % CONV_ROWS == 0 and s % GATE_ROWS == 0
    tm_in = _pick(m, (1024, 512, 256))
    tm_out = _pick(m, (512, 256))
    ts_rope = _pick(s, (512, 256))
    tq = _pick(s, (256, 128))

    cos_t, sina_t, sinb_t = _rope_tables(positions)
    w_main, w_gate = _split_w_in(w_in)
    w_pa16, w_pb16, w_out16 = w_pa.astype(BF16), w_pb.astype(BF16), w_out.astype(BF16)

    x2 = x.reshape(m, d)
    for l in range(depth):
        proj2, vb2, gates2 = _inproj(x2, norm_w[l][None], w_main[l], w_gate[l], tm_in)
        proj3 = proj2.reshape(b, s, N_F32)
        y_a = _gdn(proj3, gates2.reshape(b, s, WIDE), conv_w[l], a_log[l], dt_bias[l],
                   gdn_norm_w[l][None])
        q_r, k_r = _rope(proj3, cos_t, sina_t, sinb_t, ts_rope)
        lambda_init = 0.8 - 0.6 * math.exp(-0.3 * l)
        y_b = _attention(q_r, k_r, vb2.reshape(b, s, WIDE), proj3, diff_lambda[l],
                         diff_subln_w[l][None], lambda_init, tq)
        x2 = _outproj(x2, y_a.reshape(m, d), y_b.reshape(m, d), proj2, w_pa16[l], w_pb16[l],
                      w_out16[l], final_norm_w[None], l == depth - 1, tm_out)
    return x2.reshape(b, s, d)
```

```python
import functools
import math

import jax
import jax.numpy as jnp
from jax import lax
from jax.experimental import pallas as pl
from jax.experimental.pallas import tpu as pltpu

F32 = jnp.float32
BF16 = jnp.bfloat16

D_MODEL = 1024
HEADS = 8
HEAD_W = 128
CONV_WIDTH = 5
DIFF_DH = 64
ROPE_DIM = DIFF_DH // 4
ROPE_THETA = 500000.0
EPS = 1e-6
GDN_CHUNK = 128
LANES = 128
VMEM_LIMIT = 56 * 1024 * 1024

COL_QA, COL_KA, COL_VA, COL_ZA = 0, 8, 16, 24
COL_QB, COL_KB, COL_ZB = 32, 40, 48
COL_GA, COL_GB = 56, 64
N_F32 = 72 * LANES
WIDE = HEADS * LANES
GATE_COLS = 4 * HEADS
GATE_BETA, GATE_ALPHA = 0, 2 * HEADS


def _sigmoid(x):
    return 1.0 / (1.0 + jnp.exp(-x))


def _silu(x):
    return x * _sigmoid(x)


def _inproj_kernel(x_ref, nw_ref, w_ref, wg_ref, o_ref, vb_ref, g_ref, h_ref, *, n_f32):
    j = pl.program_id(1)

    @pl.when(j == 0)
    def _():
        x = x_ref[...]
        ms = jnp.mean(x * x, axis=-1, keepdims=True)
        hb = (x * lax.rsqrt(ms + EPS) * nw_ref[...]).astype(BF16)
        h_ref[...] = hb
        g_ref[...] = lax.dot_general(wg_ref[...], hb, (((1,), (1,)), ((), ())),
                                     preferred_element_type=F32)

    acc = jnp.dot(h_ref[...], w_ref[...], preferred_element_type=F32)

    @pl.when(j < n_f32)
    def _():
        o_ref[...] = acc

    @pl.when(j >= n_f32)
    def _():
        vb_ref[...] = acc.astype(BF16)


def _inproj(x2, norm_w, w_main, w_gate, tm, tn):
    m = x2.shape[0]
    n_f32 = N_F32 // tn
    kern = functools.partial(_inproj_kernel, n_f32=n_f32)
    return pl.pallas_call(
        kern,
        out_shape=(jax.ShapeDtypeStruct((m, N_F32), F32),
                   jax.ShapeDtypeStruct((m, WIDE), BF16),
                   jax.ShapeDtypeStruct((GATE_COLS, m), F32)),
        grid=(m // tm, n_f32 + WIDE // tn),
        in_specs=[
            pl.BlockSpec((tm, D_MODEL), lambda i, j: (i, 0)),
            pl.BlockSpec((1, D_MODEL), lambda i, j: (0, 0)),
            pl.BlockSpec((D_MODEL, tn), lambda i, j: (0, j)),
            pl.BlockSpec((GATE_COLS, D_MODEL), lambda i, j: (0, 0)),
        ],
        out_specs=(pl.BlockSpec((tm, tn), lambda i, j: (i, jnp.minimum(j, n_f32 - 1))),
                   pl.BlockSpec((tm, tn), lambda i, j: (i, jnp.maximum(j - n_f32, 0))),
                   pl.BlockSpec((GATE_COLS, tm), lambda i, j: (0, i))),
        scratch_shapes=[pltpu.VMEM((tm, D_MODEL), BF16)],
        compiler_params=pltpu.CompilerParams(
            dimension_semantics=("parallel", "arbitrary"), vmem_limit_bytes=VMEM_LIMIT),
        name="inproj",
    )(x2, norm_w, w_main, w_gate)


def _rope_kernel(q_ref, k_ref, c_ref, sa_ref, sb_ref, qo_ref, ko_ref):
    c = c_ref[0]
    sa = sa_ref[0]
    sb = sb_ref[0]
    for h in range(HEADS):
        sl = slice(h * LANES, (h + 1) * LANES)
        for src, dst, scale in ((q_ref, qo_ref, DIFF_DH ** -0.5), (k_ref, ko_ref, None)):
            t = src[0, :, sl]
            up = pltpu.roll(t, LANES - ROPE_DIM // 2, axis=1)
            dn = pltpu.roll(t, ROPE_DIM // 2, axis=1)
            r = t * c + up * sa + dn * sb
            if scale is not None:
                r = r * scale
            dst[0, :, sl] = r.astype(BF16)


def _rope(proj3, cos_t, sina_t, sinb_t, ts):
    b, s, _ = proj3.shape
    tab = pl.BlockSpec((1, ts, LANES), lambda i, j: (i, j, 0))
    return pl.pallas_call(
        _rope_kernel,
        out_shape=(jax.ShapeDtypeStruct((b, s, WIDE), BF16),
                   jax.ShapeDtypeStruct((b, s, WIDE), BF16)),
        grid=(b, s // ts),
        in_specs=[
            pl.BlockSpec((1, ts, WIDE), lambda i, j: (i, j, COL_QB // HEADS)),
            pl.BlockSpec((1, ts, WIDE), lambda i, j: (i, j, COL_KB // HEADS)),
            tab, tab, tab,
        ],
        out_specs=(pl.BlockSpec((1, ts, WIDE), lambda i, j: (i, j, 0)),
                   pl.BlockSpec((1, ts, WIDE), lambda i, j: (i, j, 0))),
        compiler_params=pltpu.CompilerParams(
            dimension_semantics=("parallel", "parallel"), vmem_limit_bytes=VMEM_LIMIT),
        name="rope",
    )(proj3, proj3, cos_t, sina_t, sinb_t)


ATTN_KEY_BLOCK = 256


def _attn_kernel(lam_ref, q_ref, k_ref, v_ref, z_ref, sw_ref, o_ref, ka_ref, kb_ref, vt_ref,
                 e_ref, m_ref, l_ref, *, lambda_init):
    @pl.when(pl.program_id(2) == 0)
    def _():
        k = k_ref[0]
        lane = lax.broadcasted_iota(jnp.int32, k.shape, 1)
        zero = jnp.zeros_like(k)
        ka_ref[...] = jnp.where(lane < DIFF_DH, k, zero)
        kb_ref[...] = jnp.where(lane >= DIFF_DH, k, zero)
        vt_ref[...] = v_ref[0].astype(F32).T.astype(BF16)

    q = q_ref[0]
    lp = lam_ref[...]
    lam = (jnp.exp(jnp.sum(lp[0:1] * lp[1:2], axis=1, keepdims=True))
           - jnp.exp(jnp.sum(lp[2:3] * lp[3:4], axis=1, keepdims=True)) + lambda_init)

    s = ka_ref.shape[0]
    tq = q.shape[0]
    tk = min(ATTN_KEY_BLOCK, s)
    nk = s // tk

    for blk in range(nk):
        rows = slice(blk * tk, (blk + 1) * tk)
        for h, kx_ref in enumerate((ka_ref, kb_ref)):
            st = lax.dot_general(kx_ref[rows, :], q, (((1,), (1,)), ((), ())),
                                 preferred_element_type=F32)
            m = jnp.max(st, axis=0, keepdims=True)
            e = jnp.exp(st - m)
            e_ref[h, rows, :] = e
            m_ref[h, blk:blk + 1, :] = m
            l_ref[h, blk:blk + 1, :] = jnp.sum(e, axis=0, keepdims=True)

    scales = []
    for h in range(2):
        m_all = m_ref[h]
        w = jnp.exp(m_all - jnp.max(m_all, axis=0, keepdims=True))
        scales.append(w * (1.0 / jnp.sum(l_ref[h] * w, axis=0, keepdims=True)))
    scales[1] = scales[1] * lam

    ot = jnp.zeros((LANES, tq), F32)
    for blk in range(nk):
        rows = slice(blk * tk, (blk + 1) * tk)
        a = (e_ref[0, rows, :] * scales[0][blk:blk + 1, :]
             - e_ref[1, rows, :] * scales[1][blk:blk + 1, :]).astype(BF16)
        ot = ot + jnp.dot(vt_ref[:, rows], a, preferred_element_type=F32)
    ms = jnp.mean(ot * ot, axis=0, keepdims=True)
    on = (ot * lax.rsqrt(ms + EPS)).T
    y = on * sw_ref[...] * (1.0 - lambda_init)
    o_ref[0] = (y * _silu(z_ref[0])).astype(BF16)


def _attention(q_r, k_r, vb3, proj3, lam_p, subln_w, lambda_init, tq):
    b, s, _ = proj3.shape
    nk = s // min(ATTN_KEY_BLOCK, s)
    kern = functools.partial(_attn_kernel, lambda_init=lambda_init)
    return pl.pallas_call(
        kern,
        out_shape=jax.ShapeDtypeStruct((b, s, WIDE), BF16),
        grid=(b, HEADS, s // tq),
        in_specs=[
            pl.BlockSpec((4, DIFF_DH), lambda i, h, j: (0, 0)),
            pl.BlockSpec((1, tq, LANES), lambda i, h, j: (i, j, h)),
            pl.BlockSpec((1, s, LANES), lambda i, h, j: (i, 0, h)),
            pl.BlockSpec((1, s, LANES), lambda i, h, j: (i, 0, h)),
            pl.BlockSpec((1, tq, LANES), lambda i, h, j: (i, j, COL_ZB + h)),
            pl.BlockSpec((1, LANES), lambda i, h, j: (0, 0)),
        ],
        out_specs=pl.BlockSpec((1, tq, LANES), lambda i, h, j: (i, j, h)),
        scratch_shapes=[pltpu.VMEM((s, LANES), BF16), pltpu.VMEM((s, LANES), BF16),
                        pltpu.VMEM((LANES, s), BF16),
                        pltpu.VMEM((2, s, tq), F32),
                        pltpu.VMEM((2, nk, tq), F32),
                        pltpu.VMEM((2, nk, tq), F32)],
        compiler_params=pltpu.CompilerParams(
            dimension_semantics=("parallel", "parallel", "arbitrary"),
            vmem_limit_bytes=VMEM_LIMIT),
        name="diff_attn",
    )(lam_p, q_r, k_r, vb3, proj3, subln_w)


CONV_ROWS = 256
OUT_ROWS = 256
HALO = 8
GV_BETA, GV_GC, GV_E1, GV_E2, GV_COUNT = 0, 2, 4, 6, 8


def _gdn_kernel(alog_ref, dtb_ref, q_ref, k_ref, v_ref, z_ref, g_ref, cwq_ref, cwk_ref, cwv_ref,
                nw_ref, o_ref,
                xp_ref, qn_ref, kn_ref, vn_ref, gv_ref, e3_ref,
                u_ref, wq_ref, a2_ref):
    s = q_ref.shape[1]
    c = GDN_CHUNK
    nc = s // c
    head = pl.program_id(1)

    assert c == LANES
    pos = lax.broadcasted_iota(jnp.int32, (nc, LANES), 1)
    for d in (0, 1):
        beta = _sigmoid(g_ref[GATE_BETA + HEADS * d + head])
        xa = g_ref[GATE_ALPHA + HEADS * d + head] + dtb_ref[d, head]
        softplus = jnp.maximum(xa, 0.0) + jnp.log(1.0 + jnp.exp(-jnp.abs(xa)))
        g = -jnp.exp(jnp.full((1, 1), alog_ref[d, head], F32)) * softplus
        pre = g
        suf = g
        sh = 1
        while sh < c:
            pre = pre + jnp.where(pos >= sh, pltpu.roll(pre, sh, axis=1), 0.0)
            suf = suf + jnp.where(pos < c - sh, pltpu.roll(suf, c - sh, axis=1), 0.0)
            sh *= 2
        gc = pre if d == 0 else suf
        rest = (suf if d == 0 else pre) - g
        gv_ref[GV_BETA + d] = beta
        gv_ref[GV_GC + d] = gc
        gv_ref[GV_E1 + d] = jnp.exp(gc)
        gv_ref[GV_E2 + d] = jnp.exp(rest)
        e3_ref[d] = jnp.exp(gc + rest)

    n_conv = s // CONV_ROWS
    pad = CONV_WIDTH // 2
    for src, cw_ref, dst, mode in ((q_ref, cwq_ref, qn_ref, "q"), (k_ref, cwk_ref, kn_ref, "k"),
                                   (v_ref, cwv_ref, vn_ref, "v")):
        cw = cw_ref[...]

        def conv_block(i, carry, src=src, cw=cw, dst=dst, mode=mode):
            r0 = pl.multiple_of(i * CONV_ROWS, CONV_ROWS)
            lo = pl.multiple_of(jnp.maximum(r0 - HALO, 0), HALO)
            hi = pl.multiple_of(jnp.minimum(r0 + CONV_ROWS, s - HALO), HALO)
            before = jnp.where(i > 0, src[0, pl.ds(lo, HALO), :], 0.0)
            after = jnp.where(i < n_conv - 1, src[0, pl.ds(hi, HALO), :], 0.0)
            xp_ref[0:HALO, :] = before
            xp_ref[HALO:HALO + CONV_ROWS, :] = src[0, pl.ds(r0, CONV_ROWS), :]
            xp_ref[HALO + CONV_ROWS:2 * HALO + CONV_ROWS, :] = after
            y = xp_ref[HALO - pad:HALO - pad + CONV_ROWS, :] * cw[0:1, :]
            for j in range(1, CONV_WIDTH):
                off = HALO - pad + j
                y = y + xp_ref[off:off + CONV_ROWS, :] * cw[j:j + 1, :]
            y = _silu(y)
            if mode != "v":
                inv = lax.rsqrt(jnp.sum(y * y, axis=-1, keepdims=True) + EPS)
                y = y * (inv * (HEAD_W ** -0.5) if mode == "q" else inv)
            dst[pl.ds(r0, CONV_ROWS), :] = y
            return carry

        lax.fori_loop(0, n_conv, conv_block, 0)

    ri = lax.broadcasted_iota(jnp.int32, (c, c), 0)
    ci = lax.broadcasted_iota(jnp.int32, (c, c), 1)
    eye = (ri == ci).astype(F32)
    masks = ((ri >= ci, ri > ci), (ri <= ci, ri < ci))
    zblk = jnp.zeros((c, c), BF16)

    def block_diag(xcat):
        x16 = xcat.astype(BF16)
        return jnp.concatenate([jnp.concatenate([x16[:, 0:c], zblk], axis=1),
                                jnp.concatenate([zblk, x16[:, c:2 * c]], axis=1)], axis=0)

    def gate_vectors(ch):
        gvr = jnp.concatenate([gv_ref[k, pl.ds(ch, 1), :] for k in range(GV_COUNT)], axis=0)
        padded = jnp.concatenate([gvr, jnp.zeros((c - GV_COUNT, c), F32)], axis=0)
        return gvr, padded.T

    def col(gvc, k):
        return gvc[:, k:k + 1]

    def chunk_group(chs):
        rows = [pl.ds(pl.multiple_of(ch * c, c), c) for ch in chs]
        n = len(chs)
        grams, qks = [], []
        for r in rows:
            kb16 = kn_ref[r, :].astype(BF16)
            grams.append(lax.dot_general(kb16, kb16, (((1,), (1,)), ((), ())),
                                         preferred_element_type=F32))
            qks.append(lax.dot_general(qn_ref[r, :].astype(BF16), kb16, (((1,), (1,)), ((), ())),
                                       preferred_element_type=F32))
        lcats = []
        for ch, r, gram, qk in zip(chs, rows, grams, qks):
            kk = kn_ref[r, :]
            qq = qn_ref[r, :]
            gvr, gvc = gate_vectors(ch)
            ls = []
            for d in (0, 1):
                incl, strict = masks[d]
                gd = col(gvc, GV_GC + d) - gvr[GV_GC + d:GV_GC + d + 1, :]
                dec = jnp.exp(jnp.where(incl, gd, 0.0))
                ls.append(jnp.where(strict, col(gvc, GV_BETA + d) * gram * dec, 0.0))
                qkm = jnp.where(incl, qk * dec, 0.0)
                a2_ref[d, ch, 0:c, :] = qkm.astype(BF16)
                a2_ref[d, ch, c:2 * c, :] = (kk * col(gvc, GV_E2 + d)).T.astype(BF16)
                wq_ref[d, ch, c:2 * c, :] = (qq * col(gvc, GV_E1 + d)).astype(BF16)
            lcats.append(jnp.concatenate(ls, axis=1))

        eye2 = jnp.concatenate([eye, eye], axis=1)
        l16s = [l.astype(BF16) for l in lcats]
        pcats = [eye2 - l for l in lcats]
        xcats = [jnp.dot(l16, block_diag(l), preferred_element_type=F32)
                 for l16, l in zip(l16s, lcats)]
        power = 2
        while power < c:
            for i in range(n):
                bd = block_diag(xcats[i])
                if 2 * power < c:
                    both = jnp.concatenate([pcats[i], xcats[i]], axis=0).astype(BF16)
                    res = jnp.dot(both, bd, preferred_element_type=F32)
                    pcats[i] = pcats[i] + res[0:c, :]
                    xcats[i] = res[c:2 * c, :]
                else:
                    pcats[i] = pcats[i] + jnp.dot(pcats[i].astype(BF16), bd,
                                                  preferred_element_type=F32)
            power *= 2

        ncats = [p - eye2 for p in pcats]
        ress = [-(nc_ + l + jnp.dot(l16, block_diag(nc_), preferred_element_type=F32))
                for nc_, l, l16 in zip(ncats, lcats, l16s)]
        pcats = [p + r_ + jnp.dot(nc_.astype(BF16), block_diag(r_), preferred_element_type=F32)
                 for p, r_, nc_ in zip(pcats, ress, ncats)]

        for ch, r, pcat in zip(chs, rows, pcats):
            kk = kn_ref[r, :]
            vv = vn_ref[r, :]
            _, gvc = gate_vectors(ch)
            for d in (0, 1):
                t16 = pcat[:, d * c:(d + 1) * c].astype(BF16)
                bcol = col(gvc, GV_BETA + d)
                rhs = jnp.concatenate([vv * bcol, kk * (bcol * col(gvc, GV_E1 + d))], axis=1)
                uw = jnp.dot(t16, rhs.astype(BF16), preferred_element_type=F32)
                u_ref[d, r, :] = uw[:, 0:LANES]
                wq_ref[d, ch, 0:c, :] = uw[:, LANES:2 * LANES].astype(BF16)

    group = max(g for g in (4, 2, 1) if nc % g == 0)

    def chunk_body(i, carry):
        chunk_group([i * group + k for k in range(group)])
        return carry

    lax.fori_loop(0, nc // group, chunk_body, 0)

    of_ref, ob_ref = qn_ref, kn_ref

    def scan_body(t, carry):
        new_states = []
        for d, out_ref in ((0, of_ref), (1, ob_ref)):
            ch = t if d == 0 else nc - 1 - t
            rows = pl.ds(pl.multiple_of(ch * c, c), c)
            st = carry[d]
            r1 = jnp.dot(wq_ref[d, ch], st.astype(BF16), preferred_element_type=F32)
            vnew = u_ref[d, rows, :] - r1[0:c, :]
            r2 = jnp.dot(a2_ref[d, ch], vnew.astype(BF16), preferred_element_type=F32)
            out_ref[rows, :] = r1[c:2 * c, :] + r2[0:c, :]
            decay = e3_ref[d, pl.ds(ch, 1), :][:, 0:1]
            new_states.append(st * decay + r2[c:2 * c, :])
        return tuple(new_states)

    zero_state = jnp.zeros((HEAD_W, HEAD_W), F32)
    lax.fori_loop(0, nc, scan_body, (zero_state, zero_state))

    def out_block(i, carry):
        rows = pl.ds(pl.multiple_of(i * OUT_ROWS, OUT_ROWS), OUT_ROWS)
        o = of_ref[rows, :] + ob_ref[rows, :]
        ms = jnp.mean(o * o, axis=-1, keepdims=True)
        y = o * lax.rsqrt(ms + EPS) * nw_ref[...]
        o_ref[0, rows, :] = (y * _silu(z_ref[0, rows, :])).astype(BF16)
        return carry

    lax.fori_loop(0, s // OUT_ROWS, out_block, 0)


def _gdn(proj3, gates3, conv_w, a_log, dt_bias, norm_w):
    b, s, _ = proj3.shape
    c = GDN_CHUNK
    nc = s // c
    col = lambda off: pl.BlockSpec((1, s, LANES), lambda i, h: (i, 0, off + h))
    cw = lambda off: pl.BlockSpec((CONV_WIDTH, LANES), lambda i, h: (0, off + h))
    smem = pl.BlockSpec(memory_space=pltpu.SMEM)
    return pl.pallas_call(
        _gdn_kernel,
        out_shape=jax.ShapeDtypeStruct((b, s, WIDE), BF16),
        grid=(b, HEADS),
        in_specs=[
            smem, smem,
            col(COL_QA), col(COL_KA), col(COL_VA), col(COL_ZA),
            pl.BlockSpec((GATE_COLS, nc, c), lambda i, h: (0, i, 0)),
            cw(0), cw(HEADS), cw(2 * HEADS),
            pl.BlockSpec((1, LANES), lambda i, h: (0, 0)),
        ],
        out_specs=pl.BlockSpec((1, s, LANES), lambda i, h: (i, 0, h)),
        scratch_shapes=[
            pltpu.VMEM((CONV_ROWS + 2 * HALO, LANES), F32),
            pltpu.VMEM((s, LANES), F32),
            pltpu.VMEM((s, LANES), F32),
            pltpu.VMEM((s, LANES), F32),
            pltpu.VMEM((GV_COUNT, nc, c), F32),
            pltpu.VMEM((2, nc, c), F32),
            pltpu.VMEM((2, s, LANES), F32),
            pltpu.VMEM((2, nc, 2 * c, LANES), BF16),
            pltpu.VMEM((2, nc, 2 * c, c), BF16),
        ],
        compiler_params=pltpu.CompilerParams(
            dimension_semantics=("parallel", "parallel"), vmem_limit_bytes=VMEM_LIMIT),
        name="gdn",
    )(a_log, dt_bias, proj3, proj3, proj3, proj3, gates3, conv_w, conv_w, conv_w, norm_w)


def _outproj_kernel(x_ref, ya_ref, yb_ref, ga_ref, gb_ref, wpa_ref, wpb_ref, wo_ref, fw_ref, o_ref,
                    *, final_norm):
    ta = jnp.dot(ya_ref[...], wpa_ref[...], preferred_element_type=F32)
    tb = jnp.dot(yb_ref[...], wpb_ref[...], preferred_element_type=F32)
    merged = _sigmoid(ga_ref[...]) * ta + _sigmoid(gb_ref[...]) * tb
    xn = x_ref[...] + jnp.dot(merged.astype(BF16), wo_ref[...], preferred_element_type=F32)
    if final_norm:
        ms = jnp.mean(xn * xn, axis=-1, keepdims=True)
        xn = xn * lax.rsqrt(ms + EPS) * fw_ref[...]
    o_ref[...] = xn


def _outproj(x2, ya2, yb2, proj2, w_pa, w_pb, w_out, final_w, final_norm, tm):
    m = x2.shape[0]
    row = lambda blk: pl.BlockSpec((tm, D_MODEL), lambda i: (i, blk))
    wfull = pl.BlockSpec((D_MODEL, D_MODEL), lambda i: (0, 0))
    kern = functools.partial(_outproj_kernel, final_norm=final_norm)
    return pl.pallas_call(
        kern,
        out_shape=jax.ShapeDtypeStruct((m, D_MODEL), F32),
        grid=(m // tm,),
        in_specs=[row(0), row(0), row(0), row(COL_GA // HEADS), row(COL_GB // HEADS),
                  wfull, wfull, wfull, pl.BlockSpec((1, D_MODEL), lambda i: (0, 0))],
        out_specs=row(0),
        compiler_params=pltpu.CompilerParams(
            dimension_semantics=("parallel",), vmem_limit_bytes=VMEM_LIMIT),
        name="outproj",
    )(x2, ya2, yb2, proj2, proj2, w_pa, w_pb, w_out, final_w)


def _pick(n, prefs):
    for p in prefs:
        if n % p == 0:
            return p
    return n


def _rope_tables(positions):
    inv_freq = ROPE_THETA ** (-(jnp.arange(0, ROPE_DIM, 2, dtype=F32) / ROPE_DIM))
    ang = positions.astype(F32)[..., None] * inv_freq
    cos, sin = jnp.cos(ang), jnp.sin(ang)
    half = ROPE_DIM // 2
    b, s, _ = ang.shape
    ones = jnp.ones((b, s, DIFF_DH - ROPE_DIM), F32)
    zeros8 = jnp.zeros((b, s, half), F32)
    zrest = jnp.zeros((b, s, DIFF_DH - ROPE_DIM), F32)
    cos64 = jnp.concatenate([cos, cos, ones], axis=-1)
    sina64 = jnp.concatenate([-sin, zeros8, zrest], axis=-1)
    sinb64 = jnp.concatenate([zeros8, sin, zrest], axis=-1)
    dup = lambda t: jnp.concatenate([t, t], axis=-1)
    return dup(cos64), dup(sina64), dup(sinb64)


def _split_w_in(w_in):
    seg = lambda k: w_in[:, :, k * WIDE:(k + 1) * WIDE]
    g0 = 3 * WIDE
    seg2 = lambda k: w_in[:, :, g0 + 4 * HEADS + k * WIDE:g0 + 4 * HEADS + (k + 1) * WIDE]
    w_main = jnp.concatenate([seg(0), seg(1), seg(2), seg2(0), seg2(1), seg2(2), seg2(4), seg2(5),
                              seg2(6), seg2(3)], axis=-1).astype(BF16)
    wg = jnp.swapaxes(w_in[:, :, g0:g0 + GATE_COLS], 1, 2)
    return w_main, wg.astype(BF16)


def kernel(x, positions, norm_w, w_in, conv_w, a_log, dt_bias, gdn_norm_w, diff_lambda,
           diff_subln_w, w_pa, w_pb, w_out, final_norm_w):
    b, s, d = x.shape
    depth = w_in.shape[0]
    m = b * s
    assert d == D_MODEL and s % GDN_CHUNK == 0 and s % CONV_ROWS == 0 and s % OUT_ROWS == 0
    tm_in = _pick(m, (2048, 1024, 512, 256))
    tn_in = 512
    tm_out = _pick(m, (512, 256))
    ts_rope = _pick(s, (512, 256))
    tq = _pick(s, (256, 128))

    cos_t, sina_t, sinb_t = _rope_tables(positions)
    w_main, w_gate = _split_w_in(w_in)
    w_pa16, w_pb16, w_out16 = w_pa.astype(BF16), w_pb.astype(BF16), w_out.astype(BF16)

    x2 = x.reshape(m, d)
    for l in range(depth):
        proj2, vb2, gates2 = _inproj(x2, norm_w[l][None], w_main[l], w_gate[l], tm_in, tn_in)
        proj3 = proj2.reshape(b, s, N_F32)
        y_a = _gdn(proj3, gates2.reshape(GATE_COLS, m // GDN_CHUNK, GDN_CHUNK), conv_w[l], a_log[l], dt_bias[l],
                   gdn_norm_w[l][None])
        q_r, k_r = _rope(proj3, cos_t, sina_t, sinb_t, ts_rope)
        lambda_init = 0.8 - 0.6 * math.exp(-0.3 * l)
        y_b = _attention(q_r, k_r, vb2.reshape(b, s, WIDE), proj3, diff_lambda[l],
                         diff_subln_w[l][None], lambda_init, tq)
        x2 = _outproj(x2, y_a.reshape(m, d), y_b.reshape(m, d), proj2, w_pa16[l], w_pb16[l],
                      w_out16[l], final_norm_w[None], l == depth - 1, tm_out)
    return x2.reshape(b, s, d)
```

```python
import functools
import math

import jax
import jax.numpy as jnp
from jax import lax
from jax.experimental import pallas as pl
from jax.experimental.pallas import tpu as pltpu

F32 = jnp.float32
BF16 = jnp.bfloat16

D_MODEL = 1024
HEADS = 8
HEAD_W = 128
CONV_WIDTH = 5
DIFF_DH = 64
ROPE_DIM = DIFF_DH // 4
ROPE_THETA = 500000.0
EPS = 1e-6
LOG2E = math.log2(math.e)
GDN_CHUNK = 128
LANES = 128
VMEM_LIMIT = 56 * 1024 * 1024

COL_QA, COL_KA, COL_VA, COL_ZA = 0, 8, 16, 24
COL_ZB, COL_GA, COL_GB = 32, 40, 48
N_F32 = 56 * LANES
COL_QB_R, COL_KB_R = 0, 8
N_ROPE = 16 * LANES
WIDE = HEADS * LANES
GATE_COLS = 4 * HEADS
GATE_BETA, GATE_ALPHA = 0, 2 * HEADS


def _sigmoid(x):
    return 1.0 / (1.0 + jnp.exp(-x))


def _silu(x):
    return x * _sigmoid(x)


def _inproj_kernel(x_ref, nw_ref, w_ref, wg_ref, c_ref, sa_ref, sb_ref,
                   o_ref, qk_ref, vb_ref, g_ref, h_ref, *, n_f32, n_rope):
    j = pl.program_id(1)

    @pl.when(j == 0)
    def _():
        x = x_ref[...]
        ms = jnp.mean(x * x, axis=-1, keepdims=True)
        hb = (x * lax.rsqrt(ms + EPS) * nw_ref[...]).astype(BF16)
        h_ref[...] = hb
        g_ref[...] = lax.dot_general(wg_ref[...], hb, (((1,), (1,)), ((), ())),
                                     preferred_element_type=F32)

    acc = jnp.dot(h_ref[...], w_ref[...], preferred_element_type=F32)

    @pl.when(j < n_f32)
    def _():
        o_ref[...] = acc

    @pl.when((j >= n_f32) & (j < n_f32 + n_rope))
    def _():
        scale = jnp.where(j < n_f32 + n_rope // 2, DIFF_DH ** -0.5 * LOG2E, 1.0)
        c = c_ref[...] * scale
        sa = sa_ref[...] * scale
        sb = sb_ref[...] * scale
        for hd in range(acc.shape[1] // LANES):
            sl = slice(hd * LANES, (hd + 1) * LANES)
            t = acc[:, sl]
            up = pltpu.roll(t, LANES - ROPE_DIM // 2, axis=1)
            dn = pltpu.roll(t, ROPE_DIM // 2, axis=1)
            qk_ref[:, sl] = (t * c + up * sa + dn * sb).astype(BF16)

    @pl.when(j >= n_f32 + n_rope)
    def _():
        vb_ref[...] = acc.astype(BF16)


def _inproj(x2, norm_w, w_main, w_gate, cos_t, sina_t, sinb_t, tm, tn):
    m = x2.shape[0]
    n_f32 = N_F32 // tn
    n_rope = N_ROPE // tn
    n_vb = WIDE // tn
    kern = functools.partial(_inproj_kernel, n_f32=n_f32, n_rope=n_rope)
    tab = pl.BlockSpec((tm, LANES), lambda i, j: (i, 0))
    return pl.pallas_call(
        kern,
        out_shape=(jax.ShapeDtypeStruct((m, N_F32), F32),
                   jax.ShapeDtypeStruct((m, N_ROPE), BF16),
                   jax.ShapeDtypeStruct((m, WIDE), BF16),
                   jax.ShapeDtypeStruct((GATE_COLS, m), F32)),
        grid=(m // tm, n_f32 + n_rope + n_vb),
        in_specs=[
            pl.BlockSpec((tm, D_MODEL), lambda i, j: (i, 0)),
            pl.BlockSpec((1, D_MODEL), lambda i, j: (0, 0)),
            pl.BlockSpec((D_MODEL, tn), lambda i, j: (0, j)),
            pl.BlockSpec((GATE_COLS, D_MODEL), lambda i, j: (0, 0)),
            tab, tab, tab,
        ],
        out_specs=(pl.BlockSpec((tm, tn), lambda i, j: (i, jnp.minimum(j, n_f32 - 1))),
                   pl.BlockSpec((tm, tn), lambda i, j: (i, jnp.clip(j - n_f32, 0, n_rope - 1))),
                   pl.BlockSpec((tm, tn),
                                lambda i, j: (i, jnp.clip(j - n_f32 - n_rope, 0, n_vb - 1))),
                   pl.BlockSpec((GATE_COLS, tm), lambda i, j: (0, i))),
        scratch_shapes=[pltpu.VMEM((tm, D_MODEL), BF16)],
        compiler_params=pltpu.CompilerParams(
            dimension_semantics=("parallel", "arbitrary"), vmem_limit_bytes=VMEM_LIMIT),
        name="inproj",
    )(x2, norm_w, w_main, w_gate, cos_t, sina_t, sinb_t)


ATTN_KEY_BLOCK = 256


def _attn_kernel(lam_ref, q_ref, k_ref, v_ref, z_ref, sw_ref, o_ref, ka_ref, kb_ref, vt_ref,
                 e_ref, m_ref, l_ref, *, lambda_init):
    @pl.when(pl.program_id(2) == 0)
    def _():
        k = k_ref[0]
        lane = lax.broadcasted_iota(jnp.int32, k.shape, 1)
        zero = jnp.zeros_like(k)
        ka_ref[...] = jnp.where(lane < DIFF_DH, k, zero)
        kb_ref[...] = jnp.where(lane >= DIFF_DH, k, zero)
        vt_ref[...] = v_ref[0].astype(F32).T.astype(BF16)

    q = q_ref[0]
    lp = lam_ref[...]
    lam = (jnp.exp(jnp.sum(lp[0:1] * lp[1:2], axis=1, keepdims=True))
           - jnp.exp(jnp.sum(lp[2:3] * lp[3:4], axis=1, keepdims=True)) + lambda_init)

    s = ka_ref.shape[0]
    tq = q.shape[0]
    tk = min(ATTN_KEY_BLOCK, s)
    nk = s // tk

    for blk in range(nk):
        rows = slice(blk * tk, (blk + 1) * tk)
        for h, kx_ref in enumerate((ka_ref, kb_ref)):
            st = lax.dot_general(kx_ref[rows, :], q, (((1,), (1,)), ((), ())),
                                 preferred_element_type=F32)
            m = jnp.max(st, axis=0, keepdims=True)
            e = jnp.exp2(st - m)
            e_ref[h, rows, :] = e
            m_ref[h, blk:blk + 1, :] = m
            l_ref[h, blk:blk + 1, :] = jnp.sum(e, axis=0, keepdims=True)

    scales = []
    for h in range(2):
        m_all = m_ref[h]
        w = jnp.exp2(m_all - jnp.max(m_all, axis=0, keepdims=True))
        scales.append(w * (1.0 / jnp.sum(l_ref[h] * w, axis=0, keepdims=True)))
    scales[1] = scales[1] * lam

    ot = jnp.zeros((LANES, tq), F32)
    for blk in range(nk):
        rows = slice(blk * tk, (blk + 1) * tk)
        a = (e_ref[0, rows, :] * scales[0][blk:blk + 1, :]
             - e_ref[1, rows, :] * scales[1][blk:blk + 1, :]).astype(BF16)
        ot = ot + jnp.dot(vt_ref[:, rows], a, preferred_element_type=F32)
    ms = jnp.mean(ot * ot, axis=0, keepdims=True)
    on = (ot * lax.rsqrt(ms + EPS)).T
    y = on * sw_ref[...] * (1.0 - lambda_init)
    o_ref[0] = (y * _silu(z_ref[0])).astype(BF16)


def _attention(qk3, vb3, proj3, lam_p, subln_w, lambda_init, tq):
    b, s, _ = proj3.shape
    nk = s // min(ATTN_KEY_BLOCK, s)
    kern = functools.partial(_attn_kernel, lambda_init=lambda_init)
    return pl.pallas_call(
        kern,
        out_shape=jax.ShapeDtypeStruct((b, s, WIDE), BF16),
        grid=(b, HEADS, s // tq),
        in_specs=[
            pl.BlockSpec((4, DIFF_DH), lambda i, h, j: (0, 0)),
            pl.BlockSpec((1, tq, LANES), lambda i, h, j: (i, j, COL_QB_R + h)),
            pl.BlockSpec((1, s, LANES), lambda i, h, j: (i, 0, COL_KB_R + h)),
            pl.BlockSpec((1, s, LANES), lambda i, h, j: (i, 0, h)),
            pl.BlockSpec((1, tq, LANES), lambda i, h, j: (i, j, COL_ZB + h)),
            pl.BlockSpec((1, LANES), lambda i, h, j: (0, 0)),
        ],
        out_specs=pl.BlockSpec((1, tq, LANES), lambda i, h, j: (i, j, h)),
        scratch_shapes=[pltpu.VMEM((s, LANES), BF16), pltpu.VMEM((s, LANES), BF16),
                        pltpu.VMEM((LANES, s), BF16),
                        pltpu.VMEM((2, s, tq), F32),
                        pltpu.VMEM((2, nk, tq), F32),
                        pltpu.VMEM((2, nk, tq), F32)],
        compiler_params=pltpu.CompilerParams(
            dimension_semantics=("parallel", "parallel", "arbitrary"),
            vmem_limit_bytes=VMEM_LIMIT),
        name="diff_attn",
    )(lam_p, qk3, qk3, vb3, proj3, subln_w)


CONV_ROWS = 256
OUT_ROWS = 256
HALO = 8
CHUNK_GROUP = 8
GV_BETA, GV_GC, GV_E1, GV_E2, GV_COUNT = 0, 2, 4, 6, 8


def _gdn_kernel(alog_ref, dtb_ref, q_ref, k_ref, v_ref, z_ref, g_ref, cwq_ref, cwk_ref, cwv_ref,
                nw_ref, o_ref,
                xp_ref, qn_ref, kn_ref, vn_ref, gv_ref, e3_ref,
                u_ref, wq_ref, a2_ref, rhs_ref):
    s = q_ref.shape[1]
    c = GDN_CHUNK
    nc = s // c
    head = pl.program_id(1)

    assert c == LANES
    pos = lax.broadcasted_iota(jnp.int32, (nc, LANES), 1)
    for d in (0, 1):
        beta = _sigmoid(g_ref[GATE_BETA + HEADS * d + head])
        xa = g_ref[GATE_ALPHA + HEADS * d + head] + dtb_ref[d, head]
        softplus = jnp.maximum(xa, 0.0) + jnp.log(1.0 + jnp.exp(-jnp.abs(xa)))
        g = -jnp.exp(jnp.full((1, 1), alog_ref[d, head], F32)) * softplus
        pre = g
        suf = g
        sh = 1
        while sh < c:
            pre = pre + jnp.where(pos >= sh, pltpu.roll(pre, sh, axis=1), 0.0)
            suf = suf + jnp.where(pos < c - sh, pltpu.roll(suf, c - sh, axis=1), 0.0)
            sh *= 2
        gc = pre if d == 0 else suf
        rest = (suf if d == 0 else pre) - g
        gv_ref[GV_BETA + d] = beta
        gv_ref[GV_GC + d] = gc
        gv_ref[GV_E1 + d] = jnp.exp(gc)
        gv_ref[GV_E2 + d] = jnp.exp(rest)
        e3_ref[d] = jnp.exp(gc + rest)

    n_conv = s // CONV_ROWS
    pad = CONV_WIDTH // 2
    for src, cw_ref, dst, mode in ((q_ref, cwq_ref, qn_ref, "q"), (k_ref, cwk_ref, kn_ref, "k"),
                                   (v_ref, cwv_ref, vn_ref, "v")):
        cw = cw_ref[...]

        def conv_block(i, carry, src=src, cw=cw, dst=dst, mode=mode):
            r0 = pl.multiple_of(i * CONV_ROWS, CONV_ROWS)
            lo = pl.multiple_of(jnp.maximum(r0 - HALO, 0), HALO)
            hi = pl.multiple_of(jnp.minimum(r0 + CONV_ROWS, s - HALO), HALO)
            before = jnp.where(i > 0, src[0, pl.ds(lo, HALO), :], 0.0)
            after = jnp.where(i < n_conv - 1, src[0, pl.ds(hi, HALO), :], 0.0)
            xp_ref[0:HALO, :] = before
            xp_ref[HALO:HALO + CONV_ROWS, :] = src[0, pl.ds(r0, CONV_ROWS), :]
            xp_ref[HALO + CONV_ROWS:2 * HALO + CONV_ROWS, :] = after
            y = xp_ref[HALO - pad:HALO - pad + CONV_ROWS, :] * cw[0:1, :]
            for j in range(1, CONV_WIDTH):
                off = HALO - pad + j
                y = y + xp_ref[off:off + CONV_ROWS, :] * cw[j:j + 1, :]
            y = _silu(y)
            if mode != "v":
                inv = lax.rsqrt(jnp.sum(y * y, axis=-1, keepdims=True) + EPS)
                y = y * (inv * (HEAD_W ** -0.5) if mode == "q" else inv)
            dst[pl.ds(r0, CONV_ROWS), :] = y
            return carry

        lax.fori_loop(0, n_conv, conv_block, 0)

    ri = lax.broadcasted_iota(jnp.int32, (c, c), 0)
    ci = lax.broadcasted_iota(jnp.int32, (c, c), 1)
    masks = ((ri >= ci, ri > ci), (ri <= ci, ri < ci))
    hc = c // 2
    quarter = lax.broadcasted_iota(jnp.int32, (hc, 2 * c), 1) // hc
    even_q = (quarter == 0) | (quarter == 2)
    eye4 = (lax.broadcasted_iota(jnp.int32, (hc, 2 * c), 0)
            == lax.broadcasted_iota(jnp.int32, (hc, 2 * c), 1) % hc).astype(F32)
    zq16 = jnp.zeros((hc, 2 * c), BF16)

    def block_diag4(x):
        x16 = x.astype(BF16)
        return jnp.concatenate([jnp.where(quarter == r, x16, zq16) for r in range(4)], axis=0)

    def gate_vectors(ch):
        gvr = jnp.concatenate([gv_ref[k, pl.ds(ch, 1), :] for k in range(GV_COUNT)], axis=0)
        padded = jnp.concatenate([gvr, jnp.zeros((c - GV_COUNT, c), F32)], axis=0)
        return gvr, padded.T

    def col(gvc, k):
        return gvc[:, k:k + 1]

    def chunk_group(chs):
        rows = [pl.ds(pl.multiple_of(ch * c, c), c) for ch in chs]
        n = len(chs)
        grams, qks = [], []
        for r in rows:
            kb16 = kn_ref[r, :].astype(BF16)
            grams.append(lax.dot_general(kb16, kb16, (((1,), (1,)), ((), ())),
                                         preferred_element_type=F32))
            qks.append(lax.dot_general(qn_ref[r, :].astype(BF16), kb16, (((1,), (1,)), ((), ())),
                                       preferred_element_type=F32))
        lcats = []
        for slot, (ch, r, gram, qk) in enumerate(zip(chs, rows, grams, qks)):
            kk = kn_ref[r, :]
            kt = kk.T
            qq = qn_ref[r, :]
            vv = vn_ref[r, :]
            gvr, gvc = gate_vectors(ch)
            ls = []
            for d in (0, 1):
                incl, strict = masks[d]
                beta_b = jnp.broadcast_to(col(gvc, GV_BETA + d), (c, LANES))
                e1_b = jnp.broadcast_to(col(gvc, GV_E1 + d), (c, LANES))
                gc_b = jnp.broadcast_to(col(gvc, GV_GC + d), (c, LANES))
                dec = jnp.exp(jnp.where(incl, gc_b - gvr[GV_GC + d:GV_GC + d + 1, :], 0.0))
                ls.append(jnp.where(strict, beta_b * gram * dec, 0.0))
                qkm = jnp.where(incl, qk * dec, 0.0)
                a2_ref[d, ch, 0:c, :] = qkm.astype(BF16)
                a2_ref[d, ch, c:2 * c, :] = (kt * gvr[GV_E2 + d:GV_E2 + d + 1, :]).astype(BF16)
                wq_ref[d, ch, c:2 * c, :] = (qq * e1_b).astype(BF16)
                rhs_ref[slot, d, :, 0:LANES] = (vv * beta_b).astype(BF16)
                rhs_ref[slot, d, :, LANES:2 * LANES] = (kk * (beta_b * e1_b)).astype(BF16)
            lcats.append(jnp.concatenate(ls, axis=1))

        xs = [jnp.where(even_q, l[0:hc, :], l[hc:c, :]) for l in lcats]
        x16s = [x.astype(BF16) for x in xs]
        ps = [eye4 - x for x in xs]
        xks = [jnp.dot(x16, block_diag4(x), preferred_element_type=F32)
               for x16, x in zip(x16s, xs)]
        power = 2
        while power < hc:
            for i in range(n):
                bd = block_diag4(xks[i])
                if 2 * power < hc:
                    both = jnp.concatenate([ps[i], xks[i]], axis=0).astype(BF16)
                    res = jnp.dot(both, bd, preferred_element_type=F32)
                    ps[i] = ps[i] + res[0:hc, :]
                    xks[i] = res[hc:c, :]
                else:
                    ps[i] = ps[i] + jnp.dot(ps[i].astype(BF16), bd, preferred_element_type=F32)
            power *= 2

        ns = [p - eye4 for p in ps]
        ress = [-(n_ + x + jnp.dot(x16, block_diag4(n_), preferred_element_type=F32))
                for n_, x, x16 in zip(ns, xs, x16s)]
        ps = [p + r_ + jnp.dot(n_.astype(BF16), block_diag4(r_), preferred_element_type=F32)
              for p, r_, n_ in zip(ps, ress, ns)]

        ys = [jnp.dot(jnp.where(quarter == 0, l[hc:c, :],
                                jnp.where(quarter == 3, l[0:hc, :], 0.0)).astype(BF16),
                      block_diag4(p), preferred_element_type=F32)
              for l, p in zip(lcats, ps)]
        pcats = []
        for p, y in zip(ps, ys):
            y16 = y.astype(BF16)
            rhs = jnp.concatenate([zq16, jnp.where(quarter == 0, y16, zq16),
                                   jnp.where(quarter == 3, y16, zq16), zq16], axis=0)
            inner = (quarter == 1) | (quarter == 2)
            zo = jnp.dot(jnp.where(inner, p, 0.0).astype(BF16), rhs, preferred_element_type=F32)
            top = jnp.where(even_q, p, jnp.where(quarter == 3, -zo, 0.0))
            bot = jnp.where(even_q, jnp.where(quarter == 0, -zo, 0.0), p)
            pcats.append(jnp.concatenate([top, bot], axis=0))

        for slot, (ch, r, pcat) in enumerate(zip(chs, rows, pcats)):
            for d in (0, 1):
                t16 = pcat[:, d * c:(d + 1) * c].astype(BF16)
                uw = jnp.dot(t16, rhs_ref[slot, d], preferred_element_type=F32)
                u_ref[d, r, :] = uw[:, 0:LANES]
                wq_ref[d, ch, 0:c, :] = uw[:, LANES:2 * LANES].astype(BF16)

    group = max(g for g in (CHUNK_GROUP, 4, 2, 1) if nc % g == 0)

    def chunk_body(i, carry):
        chunk_group([i * group + k for k in range(group)])
        return carry

    lax.fori_loop(0, nc // group, chunk_body, 0)

    of_ref, ob_ref = qn_ref, kn_ref

    def scan_body(t, carry):
        new_states = []
        for d, out_ref in ((0, of_ref), (1, ob_ref)):
            ch = t if d == 0 else nc - 1 - t
            rows = pl.ds(pl.multiple_of(ch * c, c), c)
            st = carry[d]
            r1 = jnp.dot(wq_ref[d, ch], st.astype(BF16), preferred_element_type=F32)
            vnew = u_ref[d, rows, :] - r1[0:c, :]
            r2 = jnp.dot(a2_ref[d, ch], vnew.astype(BF16), preferred_element_type=F32)
            out_ref[rows, :] = r1[c:2 * c, :] + r2[0:c, :]
            decay = e3_ref[d, pl.ds(ch, 1), :][:, 0:1]
            new_states.append(st * decay + r2[c:2 * c, :])
        return tuple(new_states)

    zero_state = jnp.zeros((HEAD_W, HEAD_W), F32)
    lax.fori_loop(0, nc, scan_body, (zero_state, zero_state))

    def out_block(i, carry):
        rows = pl.ds(pl.multiple_of(i * OUT_ROWS, OUT_ROWS), OUT_ROWS)
        o = of_ref[rows, :] + ob_ref[rows, :]
        ms = jnp.mean(o * o, axis=-1, keepdims=True)
        y = o * lax.rsqrt(ms + EPS) * nw_ref[...]
        o_ref[0, rows, :] = (y * _silu(z_ref[0, rows, :])).astype(BF16)
        return carry

    lax.fori_loop(0, s // OUT_ROWS, out_block, 0)


def _gdn(proj3, gates3, conv_w, a_log, dt_bias, norm_w):
    b, s, _ = proj3.shape
    c = GDN_CHUNK
    nc = s // c
    col = lambda off: pl.BlockSpec((1, s, LANES), lambda i, h: (i, 0, off + h))
    cw = lambda off: pl.BlockSpec((CONV_WIDTH, LANES), lambda i, h: (0, off + h))
    smem = pl.BlockSpec(memory_space=pltpu.SMEM)
    return pl.pallas_call(
        _gdn_kernel,
        out_shape=jax.ShapeDtypeStruct((b, s, WIDE), BF16),
        grid=(b, HEADS),
        in_specs=[
            smem, smem,
            col(COL_QA), col(COL_KA), col(COL_VA), col(COL_ZA),
            pl.BlockSpec((GATE_COLS, nc, c), lambda i, h: (0, i, 0)),
            cw(0), cw(HEADS), cw(2 * HEADS),
            pl.BlockSpec((1, LANES), lambda i, h: (0, 0)),
        ],
        out_specs=pl.BlockSpec((1, s, LANES), lambda i, h: (i, 0, h)),
        scratch_shapes=[
            pltpu.VMEM((CONV_ROWS + 2 * HALO, LANES), F32),
            pltpu.VMEM((s, LANES), F32),
            pltpu.VMEM((s, LANES), F32),
            pltpu.VMEM((s, LANES), F32),
            pltpu.VMEM((GV_COUNT, nc, c), F32),
            pltpu.VMEM((2, nc, c), F32),
            pltpu.VMEM((2, s, LANES), F32),
            pltpu.VMEM((2, nc, 2 * c, LANES), BF16),
            pltpu.VMEM((2, nc, 2 * c, c), BF16),
            pltpu.VMEM((CHUNK_GROUP, 2, c, 2 * LANES), BF16),
        ],
        compiler_params=pltpu.CompilerParams(
            dimension_semantics=("parallel", "parallel"), vmem_limit_bytes=VMEM_LIMIT),
        name="gdn",
    )(a_log, dt_bias, proj3, proj3, proj3, proj3, gates3, conv_w, conv_w, conv_w, norm_w)


def _outproj_kernel(x_ref, ya_ref, yb_ref, ga_ref, gb_ref, wpa_ref, wpb_ref, wo_ref, fw_ref, o_ref,
                    *, final_norm):
    ta = jnp.dot(ya_ref[...], wpa_ref[...], preferred_element_type=F32)
    tb = jnp.dot(yb_ref[...], wpb_ref[...], preferred_element_type=F32)
    merged = _sigmoid(ga_ref[...]) * ta + _sigmoid(gb_ref[...]) * tb
    xn = x_ref[...] + jnp.dot(merged.astype(BF16), wo_ref[...], preferred_element_type=F32)
    if final_norm:
        ms = jnp.mean(xn * xn, axis=-1, keepdims=True)
        xn = xn * lax.rsqrt(ms + EPS) * fw_ref[...]
    o_ref[...] = xn


def _outproj(x2, ya2, yb2, proj2, w_pa, w_pb, w_out, final_w, final_norm, tm):
    m = x2.shape[0]
    row = lambda blk: pl.BlockSpec((tm, D_MODEL), lambda i: (i, blk))
    wfull = pl.BlockSpec((D_MODEL, D_MODEL), lambda i: (0, 0))
    kern = functools.partial(_outproj_kernel, final_norm=final_norm)
    return pl.pallas_call(
        kern,
        out_shape=jax.ShapeDtypeStruct((m, D_MODEL), F32),
        grid=(m // tm,),
        in_specs=[row(0), row(0), row(0), row(COL_GA // HEADS), row(COL_GB // HEADS),
                  wfull, wfull, wfull, pl.BlockSpec((1, D_MODEL), lambda i: (0, 0))],
        out_specs=row(0),
        compiler_params=pltpu.CompilerParams(
            dimension_semantics=("parallel",), vmem_limit_bytes=VMEM_LIMIT),
        name="outproj",
    )(x2, ya2, yb2, proj2, proj2, w_pa, w_pb, w_out, final_w)


def _pick(n, prefs):
    for p in prefs:
        if n % p == 0:
            return p
    return n


def _rope_tables(positions):
    inv_freq = ROPE_THETA ** (-(jnp.arange(0, ROPE_DIM, 2, dtype=F32) / ROPE_DIM))
    ang = positions.astype(F32)[..., None] * inv_freq
    cos, sin = jnp.cos(ang), jnp.sin(ang)
    half = ROPE_DIM // 2
    b, s, _ = ang.shape
    ones = jnp.ones((b, s, DIFF_DH - ROPE_DIM), F32)
    zeros8 = jnp.zeros((b, s, half), F32)
    zrest = jnp.zeros((b, s, DIFF_DH - ROPE_DIM), F32)
    cos64 = jnp.concatenate([cos, cos, ones], axis=-1)
    sina64 = jnp.concatenate([-sin, zeros8, zrest], axis=-1)
    sinb64 = jnp.concatenate([zeros8, sin, zrest], axis=-1)
    dup = lambda t: jnp.concatenate([t, t], axis=-1)
    return dup(cos64), dup(sina64), dup(sinb64)


def _split_w_in(w_in):
    seg = lambda k: w_in[:, :, k * WIDE:(k + 1) * WIDE]
    g0 = 3 * WIDE
    seg2 = lambda k: w_in[:, :, g0 + 4 * HEADS + k * WIDE:g0 + 4 * HEADS + (k + 1) * WIDE]
    w_main = jnp.concatenate([seg(0), seg(1), seg(2), seg2(0), seg2(4), seg2(5), seg2(6),
                              seg2(1), seg2(2), seg2(3)], axis=-1).astype(BF16)
    wg = jnp.swapaxes(w_in[:, :, g0:g0 + GATE_COLS], 1, 2)
    return w_main, wg.astype(BF16)


def kernel(x, positions, norm_w, w_in, conv_w, a_log, dt_bias, gdn_norm_w, diff_lambda,
           diff_subln_w, w_pa, w_pb, w_out, final_norm_w):
    b, s, d = x.shape
    depth = w_in.shape[0]
    m = b * s
    assert d == D_MODEL and s % GDN_CHUNK == 0 and s % CONV_ROWS == 0 and s % OUT_ROWS == 0
    tm_in = _pick(m, (2048, 1024, 512, 256))
    tn_in = 512
    tm_out = _pick(m, (512, 256))
    tq = _pick(s, (512, 256, 128))

    cos_t, sina_t, sinb_t = (t.reshape(m, LANES) for t in _rope_tables(positions))
    w_main, w_gate = _split_w_in(w_in)
    w_pa16, w_pb16, w_out16 = w_pa.astype(BF16), w_pb.astype(BF16), w_out.astype(BF16)

    x2 = x.reshape(m, d)
    for l in range(depth):
        proj2, qk2, vb2, gates2 = _inproj(x2, norm_w[l][None], w_main[l], w_gate[l],
                                          cos_t, sina_t, sinb_t, tm_in, tn_in)
        proj3 = proj2.reshape(b, s, N_F32)
        y_a = _gdn(proj3, gates2.reshape(GATE_COLS, m // GDN_CHUNK, GDN_CHUNK), conv_w[l], a_log[l], dt_bias[l],
                   gdn_norm_w[l][None])
        lambda_init = 0.8 - 0.6 * math.exp(-0.3 * l)
        y_b = _attention(qk2.reshape(b, s, N_ROPE), vb2.reshape(b, s, WIDE), proj3, diff_lambda[l],
                         diff_subln_w[l][None], lambda_init, tq)
        x2 = _outproj(x2, y_a.reshape(m, d), y_b.reshape(m, d), proj2, w_pa16[l], w_pb16[l],
                      w_out16[l], final_norm_w[None], l == depth - 1, tm_out)
    return x2.reshape(b, s, d)
```

```python
import functools
import math

import jax
import jax.numpy as jnp
from jax import lax
from jax.experimental import pallas as pl
from jax.experimental.pallas import tpu as pltpu

F32 = jnp.float32
BF16 = jnp.bfloat16

D_MODEL = 1024
HEADS = 8
HEAD_W = 128
CONV_WIDTH = 5
DIFF_DH = 64
ROPE_DIM = DIFF_DH // 4
ROPE_THETA = 500000.0
EPS = 1e-6
LOG2E = math.log2(math.e)
GDN_CHUNK = 128
LANES = 128
VMEM_LIMIT = 56 * 1024 * 1024

COL_QA, COL_KA, COL_VA, COL_ZA = 0, 8, 16, 24
COL_ZB, COL_GA, COL_GB = 32, 40, 48
N_F32 = 56 * LANES
COL_QB_R, COL_KB_R = 0, 8
N_ROPE = 16 * LANES
WIDE = HEADS * LANES
GATE_COLS = 4 * HEADS
GATE_BETA, GATE_ALPHA = 0, 2 * HEADS


def _sigmoid(x):
    return 1.0 / (1.0 + jnp.exp(-x))


def _silu(x):
    return x * _sigmoid(x)


def _inproj_kernel(x_ref, nw_ref, w_ref, wg_ref, c_ref, sa_ref, sb_ref,
                   o_ref, qk_ref, vb_ref, g_ref, h_ref, *, n_f32, n_rope):
    j = pl.program_id(1)

    @pl.when(j == 0)
    def _():
        x = x_ref[...]
        ms = jnp.mean(x * x, axis=-1, keepdims=True)
        hb = (x * lax.rsqrt(ms + EPS) * nw_ref[...]).astype(BF16)
        h_ref[...] = hb
        g_ref[...] = lax.dot_general(wg_ref[...], hb, (((1,), (1,)), ((), ())),
                                     preferred_element_type=F32)

    acc = jnp.dot(h_ref[...], w_ref[...], preferred_element_type=F32)

    @pl.when(j < n_f32)
    def _():
        o_ref[...] = acc

    @pl.when((j >= n_f32) & (j < n_f32 + n_rope))
    def _():
        scale = jnp.where(j < n_f32 + n_rope // 2, DIFF_DH ** -0.5 * LOG2E, 1.0)
        c = c_ref[...] * scale
        sa = sa_ref[...] * scale
        sb = sb_ref[...] * scale
        for hd in range(acc.shape[1] // LANES):
            sl = slice(hd * LANES, (hd + 1) * LANES)
            t = acc[:, sl]
            up = pltpu.roll(t, LANES - ROPE_DIM // 2, axis=1)
            dn = pltpu.roll(t, ROPE_DIM // 2, axis=1)
            qk_ref[:, sl] = (t * c + up * sa + dn * sb).astype(BF16)

    @pl.when(j >= n_f32 + n_rope)
    def _():
        vb_ref[...] = acc.astype(BF16)


def _inproj(x2, norm_w, w_main, w_gate, cos_t, sina_t, sinb_t, tm, tn):
    m = x2.shape[0]
    n_f32 = N_F32 // tn
    n_rope = N_ROPE // tn
    n_vb = WIDE // tn
    kern = functools.partial(_inproj_kernel, n_f32=n_f32, n_rope=n_rope)
    tab = pl.BlockSpec((tm, LANES), lambda i, j: (i, 0))
    return pl.pallas_call(
        kern,
        out_shape=(jax.ShapeDtypeStruct((m, N_F32), F32),
                   jax.ShapeDtypeStruct((m, N_ROPE), BF16),
                   jax.ShapeDtypeStruct((m, WIDE), BF16),
                   jax.ShapeDtypeStruct((GATE_COLS, m), F32)),
        grid=(m // tm, n_f32 + n_rope + n_vb),
        in_specs=[
            pl.BlockSpec((tm, D_MODEL), lambda i, j: (i, 0)),
            pl.BlockSpec((1, D_MODEL), lambda i, j: (0, 0)),
            pl.BlockSpec((D_MODEL, tn), lambda i, j: (0, j)),
            pl.BlockSpec((GATE_COLS, D_MODEL), lambda i, j: (0, 0)),
            tab, tab, tab,
        ],
        out_specs=(pl.BlockSpec((tm, tn), lambda i, j: (i, jnp.minimum(j, n_f32 - 1))),
                   pl.BlockSpec((tm, tn), lambda i, j: (i, jnp.clip(j - n_f32, 0, n_rope - 1))),
                   pl.BlockSpec((tm, tn),
                                lambda i, j: (i, jnp.clip(j - n_f32 - n_rope, 0, n_vb - 1))),
                   pl.BlockSpec((GATE_COLS, tm), lambda i, j: (0, i))),
        scratch_shapes=[pltpu.VMEM((tm, D_MODEL), BF16)],
        compiler_params=pltpu.CompilerParams(
            dimension_semantics=("parallel", "arbitrary"), vmem_limit_bytes=VMEM_LIMIT),
        name="inproj",
    )(x2, norm_w, w_main, w_gate, cos_t, sina_t, sinb_t)


ATTN_KEY_BLOCK = 256


def _attn_kernel(lam_ref, q_ref, k_ref, v_ref, z_ref, sw_ref, o_ref, ka_ref, kb_ref, vt_ref,
                 e_ref, m_ref, l_ref, *, lambda_init):
    @pl.when(pl.program_id(2) == 0)
    def _():
        k = k_ref[0]
        lane = lax.broadcasted_iota(jnp.int32, k.shape, 1)
        zero = jnp.zeros_like(k)
        ka_ref[...] = jnp.where(lane < DIFF_DH, k, zero)
        kb_ref[...] = jnp.where(lane >= DIFF_DH, k, zero)
        vt_ref[...] = v_ref[0].astype(F32).T.astype(BF16)

    q = q_ref[0]
    lp = lam_ref[...]
    lam = (jnp.exp(jnp.sum(lp[0:1] * lp[1:2], axis=1, keepdims=True))
           - jnp.exp(jnp.sum(lp[2:3] * lp[3:4], axis=1, keepdims=True)) + lambda_init)

    s = ka_ref.shape[0]
    tq = q.shape[0]
    tk = min(ATTN_KEY_BLOCK, s)
    nk = s // tk

    for blk in range(nk):
        rows = slice(blk * tk, (blk + 1) * tk)
        for h, kx_ref in enumerate((ka_ref, kb_ref)):
            st = lax.dot_general(kx_ref[rows, :], q, (((1,), (1,)), ((), ())),
                                 preferred_element_type=F32)
            m = jnp.max(st, axis=0, keepdims=True)
            e = jnp.exp2(st - m)
            e_ref[h, rows, :] = e
            m_ref[h, blk:blk + 1, :] = m
            l_ref[h, blk:blk + 1, :] = jnp.sum(e, axis=0, keepdims=True)

    scales = []
    for h in range(2):
        m_all = m_ref[h]
        w = jnp.exp2(m_all - jnp.max(m_all, axis=0, keepdims=True))
        scales.append(w * (1.0 / jnp.sum(l_ref[h] * w, axis=0, keepdims=True)))
    scales[1] = scales[1] * lam

    ot = jnp.zeros((LANES, tq), F32)
    for blk in range(nk):
        rows = slice(blk * tk, (blk + 1) * tk)
        a = (e_ref[0, rows, :] * scales[0][blk:blk + 1, :]
             - e_ref[1, rows, :] * scales[1][blk:blk + 1, :]).astype(BF16)
        ot = ot + jnp.dot(vt_ref[:, rows], a, preferred_element_type=F32)
    ms = jnp.mean(ot * ot, axis=0, keepdims=True)
    on = (ot * lax.rsqrt(ms + EPS)).T
    y = on * sw_ref[...] * (1.0 - lambda_init)
    o_ref[0] = (y * _silu(z_ref[0])).astype(BF16)


def _attention(qk3, vb3, proj3, lam_p, subln_w, lambda_init, tq):
    b, s, _ = proj3.shape
    nk = s // min(ATTN_KEY_BLOCK, s)
    kern = functools.partial(_attn_kernel, lambda_init=lambda_init)
    return pl.pallas_call(
        kern,
        out_shape=jax.ShapeDtypeStruct((b, s, WIDE), BF16),
        grid=(b, HEADS, s // tq),
        in_specs=[
            pl.BlockSpec((4, DIFF_DH), lambda i, h, j: (0, 0)),
            pl.BlockSpec((1, tq, LANES), lambda i, h, j: (i, j, COL_QB_R + h)),
            pl.BlockSpec((1, s, LANES), lambda i, h, j: (i, 0, COL_KB_R + h)),
            pl.BlockSpec((1, s, LANES), lambda i, h, j: (i, 0, h)),
            pl.BlockSpec((1, tq, LANES), lambda i, h, j: (i, j, COL_ZB + h)),
            pl.BlockSpec((1, LANES), lambda i, h, j: (0, 0)),
        ],
        out_specs=pl.BlockSpec((1, tq, LANES), lambda i, h, j: (i, j, h)),
        scratch_shapes=[pltpu.VMEM((s, LANES), BF16), pltpu.VMEM((s, LANES), BF16),
                        pltpu.VMEM((LANES, s), BF16),
                        pltpu.VMEM((2, s, tq), F32),
                        pltpu.VMEM((2, nk, tq), F32),
                        pltpu.VMEM((2, nk, tq), F32)],
        compiler_params=pltpu.CompilerParams(
            dimension_semantics=("parallel", "parallel", "arbitrary"),
            vmem_limit_bytes=VMEM_LIMIT),
        name="diff_attn",
    )(lam_p, qk3, qk3, vb3, proj3, subln_w)


CONV_ROWS = 1024
OUT_ROWS = 1024
HALO = 8
CHUNK_GROUP = 8
SCAN_UNROLL = 15
GV_BETA, GV_GC, GV_E1, GV_E2, GV_COUNT = 0, 2, 4, 6, 8


def _gdn_kernel(alog_ref, dtb_ref, q_ref, k_ref, v_ref, z_ref, g_ref, cwq_ref, cwk_ref, cwv_ref,
                nw_ref, o_ref,
                xp_ref, qn_ref, kn_ref, vn_ref, gv_ref, e3_ref,
                u_ref, wq_ref, a2_ref, rhs_ref, uw16_ref, rq_ref):
    s = q_ref.shape[1]
    c = GDN_CHUNK
    nc = s // c
    head = pl.program_id(1)

    assert c == LANES
    pos = lax.broadcasted_iota(jnp.int32, (nc, LANES), 1)
    for d in (0, 1):
        beta = _sigmoid(g_ref[GATE_BETA + HEADS * d + head])
        xa = g_ref[GATE_ALPHA + HEADS * d + head] + dtb_ref[d, head]
        softplus = jnp.maximum(xa, 0.0) + jnp.log(1.0 + jnp.exp(-jnp.abs(xa)))
        g = -jnp.exp(jnp.full((1, 1), alog_ref[d, head], F32)) * softplus
        pre = g
        suf = g
        sh = 1
        while sh < c:
            pre = pre + jnp.where(pos >= sh, pltpu.roll(pre, sh, axis=1), 0.0)
            suf = suf + jnp.where(pos < c - sh, pltpu.roll(suf, c - sh, axis=1), 0.0)
            sh *= 2
        gc = pre if d == 0 else suf
        rest = (suf if d == 0 else pre) - g
        gv_ref[GV_BETA + d] = beta
        gv_ref[GV_GC + d] = gc
        gv_ref[GV_E1 + d] = jnp.exp(gc)
        gv_ref[GV_E2 + d] = jnp.exp(rest)
        e3_ref[d] = jnp.exp(gc + rest)

    cr = xp_ref.shape[0] - 2 * HALO
    n_conv = s // cr
    pad = CONV_WIDTH // 2
    for src, cw_ref, dst, mode in ((q_ref, cwq_ref, qn_ref, "q"), (k_ref, cwk_ref, kn_ref, "k"),
                                   (v_ref, cwv_ref, vn_ref, "v")):
        cw = cw_ref[...]

        def conv_block(i, carry, src=src, cw=cw, dst=dst, mode=mode):
            r0 = pl.multiple_of(i * cr, cr)
            lo = pl.multiple_of(jnp.maximum(r0 - HALO, 0), HALO)
            hi = pl.multiple_of(jnp.minimum(r0 + cr, s - HALO), HALO)
            before = jnp.where(i > 0, src[0, pl.ds(lo, HALO), :], 0.0)
            after = jnp.where(i < n_conv - 1, src[0, pl.ds(hi, HALO), :], 0.0)
            xp_ref[0:HALO, :] = before
            xp_ref[HALO:HALO + cr, :] = src[0, pl.ds(r0, cr), :]
            xp_ref[HALO + cr:2 * HALO + cr, :] = after
            y = xp_ref[HALO - pad:HALO - pad + cr, :] * cw[0:1, :]
            for j in range(1, CONV_WIDTH):
                off = HALO - pad + j
                y = y + xp_ref[off:off + cr, :] * cw[j:j + 1, :]
            y = _silu(y)
            if mode != "v":
                inv = lax.rsqrt(jnp.sum(y * y, axis=-1, keepdims=True) + EPS)
                y = y * (inv * (HEAD_W ** -0.5) if mode == "q" else inv)
            dst[pl.ds(r0, cr), :] = y
            return carry

        lax.fori_loop(0, n_conv, conv_block, 0)

    ri = lax.broadcasted_iota(jnp.int32, (c, c), 0)
    ci = lax.broadcasted_iota(jnp.int32, (c, c), 1)
    masks = ((ri >= ci, ri > ci), (ri <= ci, ri < ci))
    hc = c // 2
    quarter = lax.broadcasted_iota(jnp.int32, (hc, 2 * c), 1) // hc
    even_q = (quarter == 0) | (quarter == 2)
    eye4 = (lax.broadcasted_iota(jnp.int32, (hc, 2 * c), 0)
            == lax.broadcasted_iota(jnp.int32, (hc, 2 * c), 1) % hc).astype(F32)
    zq16 = jnp.zeros((hc, 2 * c), BF16)

    def block_diag4(x):
        x16 = x.astype(BF16)
        return jnp.concatenate([jnp.where(quarter == r, x16, zq16) for r in range(4)], axis=0)

    def gate_vectors(ch):
        gvr = jnp.concatenate([gv_ref[k, pl.ds(ch, 1), :] for k in range(GV_COUNT)], axis=0)
        padded = jnp.concatenate([gvr, jnp.zeros((c - GV_COUNT, c), F32)], axis=0)
        return gvr, padded.T

    def col(gvc, k):
        return gvc[:, k:k + 1]

    def chunk_group(chs):
        rows = [pl.ds(pl.multiple_of(ch * c, c), c) for ch in chs]
        n = len(chs)
        grams, qks = [], []
        for r in rows:
            kb16 = kn_ref[r, :].astype(BF16)
            grams.append(lax.dot_general(kb16, kb16, (((1,), (1,)), ((), ())),
                                         preferred_element_type=F32))
            qks.append(lax.dot_general(qn_ref[r, :].astype(BF16), kb16, (((1,), (1,)), ((), ())),
                                       preferred_element_type=F32))
        lcats = []
        for slot, (ch, r, gram, qk) in enumerate(zip(chs, rows, grams, qks)):
            kk = kn_ref[r, :]
            kt = kk.T
            qq = qn_ref[r, :]
            vv = vn_ref[r, :]
            gvr, gvc = gate_vectors(ch)
            ls = []
            for d in (0, 1):
                incl, strict = masks[d]
                beta_b = jnp.broadcast_to(col(gvc, GV_BETA + d), (c, LANES))
                e1_b = jnp.broadcast_to(col(gvc, GV_E1 + d), (c, LANES))
                gc_b = jnp.broadcast_to(col(gvc, GV_GC + d), (c, LANES))
                dec = jnp.exp(jnp.where(incl, gc_b - gvr[GV_GC + d:GV_GC + d + 1, :], 0.0))
                ls.append(jnp.where(strict, beta_b * gram * dec, 0.0))
                qkm = jnp.where(incl, qk * dec, 0.0)
                a2_ref[d, ch, 0:c, :] = qkm.astype(BF16)
                a2_ref[d, ch, c:2 * c, :] = (kt * gvr[GV_E2 + d:GV_E2 + d + 1, :]).astype(BF16)
                wq_ref[d, ch, c:2 * c, :] = (qq * e1_b).astype(BF16)
                rhs_ref[slot, d, :, 0:LANES] = (vv * beta_b).astype(BF16)
                rhs_ref[slot, d, :, LANES:2 * LANES] = (kk * (beta_b * e1_b)).astype(BF16)
            lcats.append(jnp.concatenate(ls, axis=1))

        xs = [jnp.where(even_q, l[0:hc, :], l[hc:c, :]) for l in lcats]
        x16s = [x.astype(BF16) for x in xs]
        ps = [eye4 - x for x in xs]
        xks = [jnp.dot(x16, block_diag4(x), preferred_element_type=F32)
               for x16, x in zip(x16s, xs)]
        power = 2
        while power < hc:
            for i in range(n):
                bd = block_diag4(xks[i])
                if 2 * power < hc:
                    both = jnp.concatenate([ps[i], xks[i]], axis=0).astype(BF16)
                    res = jnp.dot(both, bd, preferred_element_type=F32)
                    ps[i] = ps[i] + res[0:hc, :]
                    xks[i] = res[hc:c, :]
                else:
                    ps[i] = ps[i] + jnp.dot(ps[i].astype(BF16), bd, preferred_element_type=F32)
            power *= 2

        ns = [p - eye4 for p in ps]
        ress = [-(n_ + x + jnp.dot(x16, block_diag4(n_), preferred_element_type=F32))
                for n_, x, x16 in zip(ns, xs, x16s)]
        ps = [p + r_ + jnp.dot(n_.astype(BF16), block_diag4(r_), preferred_element_type=F32)
              for p, r_, n_ in zip(ps, ress, ns)]

        ys = [jnp.dot(jnp.where(quarter == 0, l[hc:c, :],
                                jnp.where(quarter == 3, l[0:hc, :], 0.0)).astype(BF16),
                      block_diag4(p), preferred_element_type=F32)
              for l, p in zip(lcats, ps)]
        pcats = []
        for p, y in zip(ps, ys):
            y16 = y.astype(BF16)
            rhs = jnp.concatenate([zq16, jnp.where(quarter == 0, y16, zq16),
                                   jnp.where(quarter == 3, y16, zq16), zq16], axis=0)
            inner = (quarter == 1) | (quarter == 2)
            zo = jnp.dot(jnp.where(inner, p, 0.0).astype(BF16), rhs, preferred_element_type=F32)
            top = jnp.where(even_q, p, jnp.where(quarter == 3, -zo, 0.0))
            bot = jnp.where(even_q, jnp.where(quarter == 0, -zo, 0.0), p)
            pcats.append(jnp.concatenate([top, bot], axis=0))

        for slot, (ch, r, pcat) in enumerate(zip(chs, rows, pcats)):
            for d in (0, 1):
                t16 = pcat[:, d * c:(d + 1) * c].astype(BF16)
                uw = jnp.dot(t16, rhs_ref[slot, d], preferred_element_type=F32)
                uw16 = uw.astype(BF16)
                u_ref[d, r, :] = uw[:, 0:LANES]
                wq_ref[d, ch, 0:c, :] = uw16[:, LANES:2 * LANES]
                uw16_ref[d, ch] = uw16

    group = max(g for g in (CHUNK_GROUP, 4, 2, 1) if nc % g == 0)

    def chunk_body(i, carry):
        chunk_group([i * group + k for k in range(group)])
        return carry

    lax.fori_loop(0, nc // group, chunk_body, 0)

    of_ref, ob_ref = qn_ref, kn_ref

    out_refs = (of_ref, ob_ref)

    def chunk_of(t, d):
        return t if d == 0 else nc - 1 - t

    def chunk_rows(ch):
        return pl.ds(pl.multiple_of(ch * c, c), c)

    def finish_output(t, d, vnew16):
        ch = chunk_of(t, d)
        out_refs[d][chunk_rows(ch), :] = rq_ref[d] + jnp.dot(
            a2_ref[d, ch, 0:c, :], vnew16, preferred_element_type=F32)

    def affine(t, d):
        ch = chunk_of(t, d)
        pp = jnp.dot(a2_ref[d, ch, c:2 * c, :], uw16_ref[d, ch], preferred_element_type=F32)
        return pp[:, 0:LANES], pp[:, LANES:2 * LANES].astype(BF16)

    def scan_step(t, carry, finish_previous, prepare_next):
        states, vprev, psis, phis = carry[0:2], carry[2:4], carry[4:6], carry[6:8]
        new_states, new_v, new_psi, new_phi = [], [], [], []
        for d in (0, 1):
            ch = chunk_of(t, d)
            st = states[d]
            st16 = st.astype(BF16)
            decay = e3_ref[d, pl.ds(ch, 1), :][:, 0:1]
            new_states.append(st * decay + psis[d]
                              - jnp.dot(phis[d], st16, preferred_element_type=F32))
            r1 = jnp.dot(wq_ref[d, ch], st16, preferred_element_type=F32)
            if finish_previous:
                finish_output(t - 1, d, vprev[d])
            rq_ref[d] = r1[c:2 * c, :]
            new_v.append((u_ref[d, chunk_rows(ch), :] - r1[0:c, :]).astype(BF16))
            psi, phi = affine(t + 1, d) if prepare_next else (psis[d], phis[d])
            new_psi.append(psi)
            new_phi.append(phi)
        return tuple(new_states) + tuple(new_v) + tuple(new_psi) + tuple(new_phi)

    zero_state = jnp.zeros((HEAD_W, HEAD_W), F32)
    zero_v = jnp.zeros((c, HEAD_W), BF16)
    first = [affine(0, d) for d in (0, 1)]
    carry = (zero_state, zero_state, zero_v, zero_v,
             first[0][0], first[1][0], first[0][1], first[1][1])
    if nc > 1:
        carry = scan_step(0, carry, False, True)
        carry = lax.fori_loop(1, nc - 1, lambda t, cr: scan_step(t, cr, True, True), carry,
                              unroll=max(1, min(SCAN_UNROLL, nc - 2)))
    carry = scan_step(nc - 1, carry, nc > 1, False)
    for d in (0, 1):
        finish_output(nc - 1, d, carry[2 + d])

    orows = min(OUT_ROWS, s)

    def out_block(i, carry):
        rows = pl.ds(pl.multiple_of(i * orows, orows), orows)
        o = of_ref[rows, :] + ob_ref[rows, :]
        ms = jnp.mean(o * o, axis=-1, keepdims=True)
        y = o * lax.rsqrt(ms + EPS) * nw_ref[...]
        o_ref[0, rows, :] = (y * _silu(z_ref[0, rows, :])).astype(BF16)
        return carry

    lax.fori_loop(0, s // orows, out_block, 0)


def _gdn(proj3, gates3, conv_w, a_log, dt_bias, norm_w):
    b, s, _ = proj3.shape
    c = GDN_CHUNK
    nc = s // c
    col = lambda off: pl.BlockSpec((1, s, LANES), lambda i, h: (i, 0, off + h))
    cw = lambda off: pl.BlockSpec((CONV_WIDTH, LANES), lambda i, h: (0, off + h))
    smem = pl.BlockSpec(memory_space=pltpu.SMEM)
    return pl.pallas_call(
        _gdn_kernel,
        out_shape=jax.ShapeDtypeStruct((b, s, WIDE), BF16),
        grid=(b, HEADS),
        in_specs=[
            smem, smem,
            col(COL_QA), col(COL_KA), col(COL_VA), col(COL_ZA),
            pl.BlockSpec((GATE_COLS, nc, c), lambda i, h: (0, i, 0)),
            cw(0), cw(HEADS), cw(2 * HEADS),
            pl.BlockSpec((1, LANES), lambda i, h: (0, 0)),
        ],
        out_specs=pl.BlockSpec((1, s, LANES), lambda i, h: (i, 0, h)),
        scratch_shapes=[
            pltpu.VMEM((min(CONV_ROWS, s) + 2 * HALO, LANES), F32),
            pltpu.VMEM((s, LANES), F32),
            pltpu.VMEM((s, LANES), F32),
            pltpu.VMEM((s, LANES), F32),
            pltpu.VMEM((GV_COUNT, nc, c), F32),
            pltpu.VMEM((2, nc, c), F32),
            pltpu.VMEM((2, s, LANES), F32),
            pltpu.VMEM((2, nc, 2 * c, LANES), BF16),
            pltpu.VMEM((2, nc, 2 * c, c), BF16),
            pltpu.VMEM((CHUNK_GROUP, 2, c, 2 * LANES), BF16),
            pltpu.VMEM((2, nc, c, 2 * LANES), BF16),
            pltpu.VMEM((2, c, HEAD_W), F32),
        ],
        compiler_params=pltpu.CompilerParams(
            dimension_semantics=("parallel", "parallel"), vmem_limit_bytes=VMEM_LIMIT),
        name="gdn",
    )(a_log, dt_bias, proj3, proj3, proj3, proj3, gates3, conv_w, conv_w, conv_w, norm_w)


def _outproj_kernel(x_ref, ya_ref, yb_ref, ga_ref, gb_ref, wpa_ref, wpb_ref, wo_ref, fw_ref, o_ref,
                    *, final_norm):
    ta = jnp.dot(ya_ref[...], wpa_ref[...], preferred_element_type=F32)
    tb = jnp.dot(yb_ref[...], wpb_ref[...], preferred_element_type=F32)
    merged = _sigmoid(ga_ref[...]) * ta + _sigmoid(gb_ref[...]) * tb
    xn = x_ref[...] + jnp.dot(merged.astype(BF16), wo_ref[...], preferred_element_type=F32)
    if final_norm:
        ms = jnp.mean(xn * xn, axis=-1, keepdims=True)
        xn = xn * lax.rsqrt(ms + EPS) * fw_ref[...]
    o_ref[...] = xn


def _outproj(x2, ya2, yb2, proj2, w_pa, w_pb, w_out, final_w, final_norm, tm):
    m = x2.shape[0]
    row = lambda blk: pl.BlockSpec((tm, D_MODEL), lambda i: (i, blk))
    wfull = pl.BlockSpec((D_MODEL, D_MODEL), lambda i: (0, 0))
    kern = functools.partial(_outproj_kernel, final_norm=final_norm)
    return pl.pallas_call(
        kern,
        out_shape=jax.ShapeDtypeStruct((m, D_MODEL), F32),
        grid=(m // tm,),
        in_specs=[row(0), row(0), row(0), row(COL_GA // HEADS), row(COL_GB // HEADS),
                  wfull, wfull, wfull, pl.BlockSpec((1, D_MODEL), lambda i: (0, 0))],
        out_specs=row(0),
        compiler_params=pltpu.CompilerParams(
            dimension_semantics=("parallel",), vmem_limit_bytes=VMEM_LIMIT),
        name="outproj",
    )(x2, ya2, yb2, proj2, proj2, w_pa, w_pb, w_out, final_w)


def _pick(n, prefs):
    for p in prefs:
        if n % p == 0:
            return p
    return n


def _rope_tables(positions):
    inv_freq = ROPE_THETA ** (-(jnp.arange(0, ROPE_DIM, 2, dtype=F32) / ROPE_DIM))
    ang = positions.astype(F32)[..., None] * inv_freq
    cos, sin = jnp.cos(ang), jnp.sin(ang)
    half = ROPE_DIM // 2
    b, s, _ = ang.shape
    ones = jnp.ones((b, s, DIFF_DH - ROPE_DIM), F32)
    zeros8 = jnp.zeros((b, s, half), F32)
    zrest = jnp.zeros((b, s, DIFF_DH - ROPE_DIM), F32)
    cos64 = jnp.concatenate([cos, cos, ones], axis=-1)
    sina64 = jnp.concatenate([-sin, zeros8, zrest], axis=-1)
    sinb64 = jnp.concatenate([zeros8, sin, zrest], axis=-1)
    dup = lambda t: jnp.concatenate([t, t], axis=-1)
    return dup(cos64), dup(sina64), dup(sinb64)


def _split_w_in(w_in):
    seg = lambda k: w_in[:, :, k * WIDE:(k + 1) * WIDE]
    g0 = 3 * WIDE
    seg2 = lambda k: w_in[:, :, g0 + 4 * HEADS + k * WIDE:g0 + 4 * HEADS + (k + 1) * WIDE]
    w_main = jnp.concatenate([seg(0), seg(1), seg(2), seg2(0), seg2(4), seg2(5), seg2(6),
                              seg2(1), seg2(2), seg2(3)], axis=-1).astype(BF16)
    wg = jnp.swapaxes(w_in[:, :, g0:g0 + GATE_COLS], 1, 2)
    return w_main, wg.astype(BF16)


def kernel(x, positions, norm_w, w_in, conv_w, a_log, dt_bias, gdn_norm_w, diff_lambda,
           diff_subln_w, w_pa, w_pb, w_out, final_norm_w):
    b, s, d = x.shape
    depth = w_in.shape[0]
    m = b * s
    assert d == D_MODEL and s % GDN_CHUNK == 0 and s % min(CONV_ROWS, s) == 0 and s % min(OUT_ROWS, s) == 0
    tm_in = _pick(m, (2048, 1024, 512, 256))
    tn_in = 512
    tm_out = _pick(m, (512, 256))
    tq = _pick(s, (512, 256, 128))

    cos_t, sina_t, sinb_t = (t.reshape(m, LANES) for t in _rope_tables(positions))
    w_main, w_gate = _split_w_in(w_in)
    w_pa16, w_pb16, w_out16 = w_pa.astype(BF16), w_pb.astype(BF16), w_out.astype(BF16)

    x2 = x.reshape(m, d)
    for l in range(depth):
        proj2, qk2, vb2, gates2 = _inproj(x2, norm_w[l][None], w_main[l], w_gate[l],
                                          cos_t, sina_t, sinb_t, tm_in, tn_in)
        proj3 = proj2.reshape(b, s, N_F32)
        y_a = _gdn(proj3, gates2.reshape(GATE_COLS, m // GDN_CHUNK, GDN_CHUNK), conv_w[l], a_log[l], dt_bias[l],
                   gdn_norm_w[l][None])
        lambda_init = 0.8 - 0.6 * math.exp(-0.3 * l)
        y_b = _attention(qk2.reshape(b, s, N_ROPE), vb2.reshape(b, s, WIDE), proj3, diff_lambda[l],
                         diff_subln_w[l][None], lambda_init, tq)
        x2 = _outproj(x2, y_a.reshape(m, d), y_b.reshape(m, d), proj2, w_pa16[l], w_pb16[l],
                      w_out16[l], final_norm_w[None], l == depth - 1, tm_out)
    return x2.reshape(b, s, d)
```

```python
import functools
import math

import jax
import jax.numpy as jnp
from jax import lax
from jax.experimental import pallas as pl
from jax.experimental.pallas import tpu as pltpu

F32 = jnp.float32
BF16 = jnp.bfloat16

D_MODEL = 1024
HEADS = 8
HEAD_W = 128
CONV_WIDTH = 5
DIFF_DH = 64
ROPE_DIM = DIFF_DH // 4
ROPE_THETA = 500000.0
EPS = 1e-6
LOG2E = math.log2(math.e)
GDN_CHUNK = 128
LANES = 128
VMEM_LIMIT = 56 * 1024 * 1024

COL_QA, COL_KA, COL_VA, COL_ZA = 0, 8, 16, 24
COL_ZB, COL_GA, COL_GB = 32, 40, 48
N_F32 = 56 * LANES
COL_QB_R, COL_KB_R = 0, 8
N_ROPE = 16 * LANES
WIDE = HEADS * LANES
GATE_COLS = 4 * HEADS
GATE_BETA, GATE_ALPHA = 0, 2 * HEADS


def _sigmoid(x):
    return 1.0 / (1.0 + jnp.exp(-x))


def _silu(x):
    return x * _sigmoid(x)


def _inproj_kernel(x_ref, nw_ref, w_ref, wg_ref, c_ref, sa_ref, sb_ref,
                   o_ref, qk_ref, vb_ref, g_ref, h_ref, *, n_f32, n_rope):
    j = pl.program_id(1)

    @pl.when(j == 0)
    def _():
        x = x_ref[...]
        ms = jnp.mean(x * x, axis=-1, keepdims=True)
        hb = (x * lax.rsqrt(ms + EPS) * nw_ref[...]).astype(BF16)
        h_ref[...] = hb
        g_ref[...] = lax.dot_general(wg_ref[...], hb, (((1,), (1,)), ((), ())),
                                     preferred_element_type=F32)

    @pl.when(j < n_f32)
    def _():
        o_ref[...] = jnp.dot(h_ref[...], w_ref[...], preferred_element_type=F32)

    @pl.when((j >= n_f32) & (j < n_f32 + n_rope))
    def _():
        scale = jnp.where(j < n_f32 + n_rope // 2, DIFF_DH ** -0.5 * LOG2E, 1.0)
        c = c_ref[...] * scale
        sa = sa_ref[...] * scale
        sb = sb_ref[...] * scale
        acc = jnp.dot(h_ref[...], w_ref[...], preferred_element_type=F32)
        for hd in range(acc.shape[1] // LANES):
            sl = slice(hd * LANES, (hd + 1) * LANES)
            t = acc[:, sl]
            up = pltpu.roll(t, LANES - ROPE_DIM // 2, axis=1)
            dn = pltpu.roll(t, ROPE_DIM // 2, axis=1)
            qk_ref[:, sl] = (t * c + up * sa + dn * sb).astype(BF16)

    @pl.when(j >= n_f32 + n_rope)
    def _():
        vb_ref[...] = jnp.dot(h_ref[...], w_ref[...], preferred_element_type=F32).astype(BF16)


def _inproj(x2, norm_w, w_main, w_gate, cos_t, sina_t, sinb_t, tm, tn):
    m = x2.shape[0]
    n_f32 = N_F32 // tn
    n_rope = N_ROPE // tn
    n_vb = WIDE // tn
    kern = functools.partial(_inproj_kernel, n_f32=n_f32, n_rope=n_rope)
    tab = pl.BlockSpec((tm, LANES), lambda i, j: (i, 0))
    return pl.pallas_call(
        kern,
        out_shape=(jax.ShapeDtypeStruct((m, N_F32), F32),
                   jax.ShapeDtypeStruct((m, N_ROPE), BF16),
                   jax.ShapeDtypeStruct((m, WIDE), BF16),
                   jax.ShapeDtypeStruct((GATE_COLS, m), F32)),
        grid=(m // tm, n_f32 + n_rope + n_vb),
        in_specs=[
            pl.BlockSpec((tm, D_MODEL), lambda i, j: (i, 0)),
            pl.BlockSpec((1, D_MODEL), lambda i, j: (0, 0)),
            pl.BlockSpec((D_MODEL, tn), lambda i, j: (0, j)),
            pl.BlockSpec((GATE_COLS, D_MODEL), lambda i, j: (0, 0)),
            tab, tab, tab,
        ],
        out_specs=(pl.BlockSpec((tm, tn), lambda i, j: (i, jnp.minimum(j, n_f32 - 1))),
                   pl.BlockSpec((tm, tn), lambda i, j: (i, jnp.clip(j - n_f32, 0, n_rope - 1))),
                   pl.BlockSpec((tm, tn),
                                lambda i, j: (i, jnp.clip(j - n_f32 - n_rope, 0, n_vb - 1))),
                   pl.BlockSpec((GATE_COLS, tm), lambda i, j: (0, i))),
        scratch_shapes=[pltpu.VMEM((tm, D_MODEL), BF16)],
        compiler_params=pltpu.CompilerParams(
            dimension_semantics=("parallel", "arbitrary"), vmem_limit_bytes=VMEM_LIMIT),
        name="inproj",
    )(x2, norm_w, w_main, w_gate, cos_t, sina_t, sinb_t)


ATTN_KEY_BLOCK = 256


def _attn_kernel(lam_ref, q_ref, k_ref, v_ref, z_ref, sw_ref, o_ref, ka_ref, kb_ref, vt_ref,
                 e_ref, m_ref, l_ref, *, lambda_init):
    @pl.when(pl.program_id(2) == 0)
    def _():
        k = k_ref[0]
        lane = lax.broadcasted_iota(jnp.int32, k.shape, 1)
        zero = jnp.zeros_like(k)
        ka_ref[...] = jnp.where(lane < DIFF_DH, k, zero)
        kb_ref[...] = jnp.where(lane >= DIFF_DH, k, zero)
        vt_ref[...] = v_ref[0].astype(F32).T.astype(BF16)

    q = q_ref[0]
    lp = lam_ref[...]
    lam = (jnp.exp(jnp.sum(lp[0:1] * lp[1:2], axis=1, keepdims=True))
           - jnp.exp(jnp.sum(lp[2:3] * lp[3:4], axis=1, keepdims=True)) + lambda_init)

    s = ka_ref.shape[0]
    tq = q.shape[0]
    tk = min(ATTN_KEY_BLOCK, s)
    nk = s // tk

    for blk in range(nk):
        rows = slice(blk * tk, (blk + 1) * tk)
        for h, kx_ref in enumerate((ka_ref, kb_ref)):
            st = lax.dot_general(kx_ref[rows, :], q, (((1,), (1,)), ((), ())),
                                 preferred_element_type=F32)
            m = jnp.max(st, axis=0, keepdims=True)
            e = jnp.exp2(st - m)
            e_ref[h, rows, :] = e
            m_ref[h, blk:blk + 1, :] = m
            l_ref[h, blk:blk + 1, :] = jnp.sum(e, axis=0, keepdims=True)

    scales = []
    for h in range(2):
        m_all = m_ref[h]
        w = jnp.exp2(m_all - jnp.max(m_all, axis=0, keepdims=True))
        scales.append(w * (1.0 / jnp.sum(l_ref[h] * w, axis=0, keepdims=True)))
    scales[1] = scales[1] * lam

    ot = jnp.zeros((LANES, tq), F32)
    for blk in range(nk):
        rows = slice(blk * tk, (blk + 1) * tk)
        a = (e_ref[0, rows, :] * scales[0][blk:blk + 1, :]
             - e_ref[1, rows, :] * scales[1][blk:blk + 1, :]).astype(BF16)
        ot = ot + jnp.dot(vt_ref[:, rows], a, preferred_element_type=F32)
    ms = jnp.mean(ot * ot, axis=0, keepdims=True)
    on = (ot * lax.rsqrt(ms + EPS)).T
    y = on * sw_ref[...] * (1.0 - lambda_init)
    o_ref[0] = (y * _silu(z_ref[0])).astype(BF16)


def _attention(qk3, vb3, proj3, lam_p, subln_w, lambda_init, tq):
    b, s, _ = proj3.shape
    nk = s // min(ATTN_KEY_BLOCK, s)
    kern = functools.partial(_attn_kernel, lambda_init=lambda_init)
    return pl.pallas_call(
        kern,
        out_shape=jax.ShapeDtypeStruct((b, s, WIDE), BF16),
        grid=(b, HEADS, s // tq),
        in_specs=[
            pl.BlockSpec((4, DIFF_DH), lambda i, h, j: (0, 0)),
            pl.BlockSpec((1, tq, LANES), lambda i, h, j: (i, j, COL_QB_R + h)),
            pl.BlockSpec((1, s, LANES), lambda i, h, j: (i, 0, COL_KB_R + h)),
            pl.BlockSpec((1, s, LANES), lambda i, h, j: (i, 0, h)),
            pl.BlockSpec((1, tq, LANES), lambda i, h, j: (i, j, COL_ZB + h)),
            pl.BlockSpec((1, LANES), lambda i, h, j: (0, 0)),
        ],
        out_specs=pl.BlockSpec((1, tq, LANES), lambda i, h, j: (i, j, h)),
        scratch_shapes=[pltpu.VMEM((s, LANES), BF16), pltpu.VMEM((s, LANES), BF16),
                        pltpu.VMEM((LANES, s), BF16),
                        pltpu.VMEM((2, s, tq), F32),
                        pltpu.VMEM((2, nk, tq), F32),
                        pltpu.VMEM((2, nk, tq), F32)],
        compiler_params=pltpu.CompilerParams(
            dimension_semantics=("parallel", "parallel", "arbitrary"),
            vmem_limit_bytes=VMEM_LIMIT),
        name="diff_attn",
    )(lam_p, qk3, qk3, vb3, proj3, subln_w)


CONV_ROWS = 1024
OUT_ROWS = 1024
HALO = 8
CHUNK_GROUP = 8
SCAN_UNROLL = 15
GV_BETA, GV_GC, GV_E1, GV_E2, GV_COUNT = 0, 2, 4, 6, 8


def _gdn_kernel(alog_ref, dtb_ref, q_ref, k_ref, v_ref, z_ref, g_ref, cwq_ref, cwk_ref, cwv_ref,
                nw_ref, o_ref,
                xp_ref, qn_ref, kn_ref, vn_ref, gv_ref, e3_ref,
                u_ref, wq_ref, a2_ref, rhs_ref, uw16_ref, rq_ref):
    s = q_ref.shape[1]
    c = GDN_CHUNK
    nc = s // c
    head = pl.program_id(1)

    assert c == LANES
    pos = lax.broadcasted_iota(jnp.int32, (nc, LANES), 1)
    for d in (0, 1):
        beta = _sigmoid(g_ref[GATE_BETA + HEADS * d + head])
        xa = g_ref[GATE_ALPHA + HEADS * d + head] + dtb_ref[d, head]
        softplus = jnp.maximum(xa, 0.0) + jnp.log(1.0 + jnp.exp(-jnp.abs(xa)))
        g = -jnp.exp(jnp.full((1, 1), alog_ref[d, head], F32)) * softplus
        pre = g
        suf = g
        sh = 1
        while sh < c:
            pre = pre + jnp.where(pos >= sh, pltpu.roll(pre, sh, axis=1), 0.0)
            suf = suf + jnp.where(pos < c - sh, pltpu.roll(suf, c - sh, axis=1), 0.0)
            sh *= 2
        gc = pre if d == 0 else suf
        rest = (suf if d == 0 else pre) - g
        gv_ref[GV_BETA + d] = beta
        gv_ref[GV_GC + d] = gc
        gv_ref[GV_E1 + d] = jnp.exp(gc)
        gv_ref[GV_E2 + d] = jnp.exp(rest)
        e3_ref[d] = jnp.exp(gc + rest)

    cr = xp_ref.shape[0] - 2 * HALO
    n_conv = s // cr
    pad = CONV_WIDTH // 2
    for src, cw_ref, dst, mode in ((q_ref, cwq_ref, qn_ref, "q"), (k_ref, cwk_ref, kn_ref, "k"),
                                   (v_ref, cwv_ref, vn_ref, "v")):
        cw = cw_ref[...]

        def conv_block(i, carry, src=src, cw=cw, dst=dst, mode=mode):
            r0 = pl.multiple_of(i * cr, cr)
            lo = pl.multiple_of(jnp.maximum(r0 - HALO, 0), HALO)
            hi = pl.multiple_of(jnp.minimum(r0 + cr, s - HALO), HALO)
            before = jnp.where(i > 0, src[0, pl.ds(lo, HALO), :], 0.0)
            after = jnp.where(i < n_conv - 1, src[0, pl.ds(hi, HALO), :], 0.0)
            xp_ref[0:HALO, :] = before
            xp_ref[HALO:HALO + cr, :] = src[0, pl.ds(r0, cr), :]
            xp_ref[HALO + cr:2 * HALO + cr, :] = after
            y = xp_ref[HALO - pad:HALO - pad + cr, :] * cw[0:1, :]
            for j in range(1, CONV_WIDTH):
                off = HALO - pad + j
                y = y + xp_ref[off:off + cr, :] * cw[j:j + 1, :]
            y = _silu(y)
            if mode != "v":
                inv = lax.rsqrt(jnp.sum(y * y, axis=-1, keepdims=True) + EPS)
                y = y * (inv * (HEAD_W ** -0.5) if mode == "q" else inv)
            dst[pl.ds(r0, cr), :] = y
            return carry

        lax.fori_loop(0, n_conv, conv_block, 0)

    ri = lax.broadcasted_iota(jnp.int32, (c, c), 0)
    ci = lax.broadcasted_iota(jnp.int32, (c, c), 1)
    masks = ((ri >= ci, ri > ci), (ri <= ci, ri < ci))
    hc = c // 2
    quarter = lax.broadcasted_iota(jnp.int32, (hc, 2 * c), 1) // hc
    even_q = (quarter == 0) | (quarter == 2)
    eye4 = (lax.broadcasted_iota(jnp.int32, (hc, 2 * c), 0)
            == lax.broadcasted_iota(jnp.int32, (hc, 2 * c), 1) % hc).astype(F32)
    zq16 = jnp.zeros((hc, 2 * c), BF16)

    def block_diag4(x):
        x16 = x.astype(BF16)
        return jnp.concatenate([jnp.where(quarter == r, x16, zq16) for r in range(4)], axis=0)

    def gate_vectors(ch):
        gvr = jnp.concatenate([gv_ref[k, pl.ds(ch, 1), :] for k in range(GV_COUNT)], axis=0)
        padded = jnp.concatenate([gvr, jnp.zeros((c - GV_COUNT, c), F32)], axis=0)
        return gvr, padded.T

    def col(gvc, k):
        return gvc[:, k:k + 1]

    def chunk_group(chs):
        rows = [pl.ds(pl.multiple_of(ch * c, c), c) for ch in chs]
        n = len(chs)
        grams, qks = [], []
        for r in rows:
            kb16 = kn_ref[r, :].astype(BF16)
            grams.append(lax.dot_general(kb16, kb16, (((1,), (1,)), ((), ())),
                                         preferred_element_type=F32))
            qks.append(lax.dot_general(qn_ref[r, :].astype(BF16), kb16, (((1,), (1,)), ((), ())),
                                       preferred_element_type=F32))
        lcats = []
        for slot, (ch, r, gram, qk) in enumerate(zip(chs, rows, grams, qks)):
            kk = kn_ref[r, :]
            kt = kk.T
            qq = qn_ref[r, :]
            vv = vn_ref[r, :]
            gvr, gvc = gate_vectors(ch)
            ls = []
            for d in (0, 1):
                incl, strict = masks[d]
                beta_b = jnp.broadcast_to(col(gvc, GV_BETA + d), (c, LANES))
                e1_b = jnp.broadcast_to(col(gvc, GV_E1 + d), (c, LANES))
                gc_b = jnp.broadcast_to(col(gvc, GV_GC + d), (c, LANES))
                dec = jnp.exp(jnp.where(incl, gc_b - gvr[GV_GC + d:GV_GC + d + 1, :], 0.0))
                ls.append(jnp.where(strict, beta_b * gram * dec, 0.0))
                qkm = jnp.where(incl, qk * dec, 0.0)
                a2_ref[d, ch, 0:c, :] = qkm.astype(BF16)
                a2_ref[d, ch, c:2 * c, :] = (kt * gvr[GV_E2 + d:GV_E2 + d + 1, :]).astype(BF16)
                wq_ref[d, ch, c:2 * c, :] = (qq * e1_b).astype(BF16)
                rhs_ref[slot, d, :, 0:LANES] = (vv * beta_b).astype(BF16)
                rhs_ref[slot, d, :, LANES:2 * LANES] = (kk * (beta_b * e1_b)).astype(BF16)
            lcats.append(jnp.concatenate(ls, axis=1))

        xs = [jnp.where(even_q, l[0:hc, :], l[hc:c, :]) for l in lcats]
        x16s = [x.astype(BF16) for x in xs]
        ps = [eye4 - x for x in xs]
        xks = [jnp.dot(x16, block_diag4(x), preferred_element_type=F32)
               for x16, x in zip(x16s, xs)]
        power = 2
        while power < hc:
            for i in range(n):
                bd = block_diag4(xks[i])
                if 2 * power < hc:
                    both = jnp.concatenate([ps[i], xks[i]], axis=0).astype(BF16)
                    res = jnp.dot(both, bd, preferred_element_type=F32)
                    ps[i] = ps[i] + res[0:hc, :]
                    xks[i] = res[hc:c, :]
                else:
                    ps[i] = ps[i] + jnp.dot(ps[i].astype(BF16), bd, preferred_element_type=F32)
            power *= 2

        ns = [p - eye4 for p in ps]
        ress = [-(n_ + x + jnp.dot(x16, block_diag4(n_), preferred_element_type=F32))
                for n_, x, x16 in zip(ns, xs, x16s)]
        ps = [p + r_ + jnp.dot(n_.astype(BF16), block_diag4(r_), preferred_element_type=F32)
              for p, r_, n_ in zip(ps, ress, ns)]

        ys = [jnp.dot(jnp.where(quarter == 0, l[hc:c, :],
                                jnp.where(quarter == 3, l[0:hc, :], 0.0)).astype(BF16),
                      block_diag4(p), preferred_element_type=F32)
              for l, p in zip(lcats, ps)]
        pcats = []
        for p, y in zip(ps, ys):
            y16 = y.astype(BF16)
            rhs = jnp.concatenate([zq16, jnp.where(quarter == 0, y16, zq16),
                                   jnp.where(quarter == 3, y16, zq16), zq16], axis=0)
            inner = (quarter == 1) | (quarter == 2)
            zo = jnp.dot(jnp.where(inner, p, 0.0).astype(BF16), rhs, preferred_element_type=F32)
            top = jnp.where(even_q, p, jnp.where(quarter == 3, -zo, 0.0))
            bot = jnp.where(even_q, jnp.where(quarter == 0, -zo, 0.0), p)
            pcats.append(jnp.concatenate([top, bot], axis=0))

        for slot, (ch, r, pcat) in enumerate(zip(chs, rows, pcats)):
            for d in (0, 1):
                t16 = pcat[:, d * c:(d + 1) * c].astype(BF16)
                uw = jnp.dot(t16, rhs_ref[slot, d], preferred_element_type=F32)
                uw16 = uw.astype(BF16)
                u_ref[d, r, :] = uw[:, 0:LANES]
                wq_ref[d, ch, 0:c, :] = uw16[:, LANES:2 * LANES]
                uw16_ref[d, ch] = uw16

    group = max(g for g in (CHUNK_GROUP, 4, 2, 1) if nc % g == 0)

    def chunk_body(i, carry):
        chunk_group([i * group + k for k in range(group)])
        return carry

    lax.fori_loop(0, nc // group, chunk_body, 0)

    of_ref, ob_ref = qn_ref, kn_ref

    out_refs = (of_ref, ob_ref)

    def chunk_of(t, d):
        return t if d == 0 else nc - 1 - t

    def chunk_rows(ch):
        return pl.ds(pl.multiple_of(ch * c, c), c)

    def finish_output(t, d, vnew16):
        ch = chunk_of(t, d)
        out_refs[d][chunk_rows(ch), :] = rq_ref[d] + jnp.dot(
            a2_ref[d, ch, 0:c, :], vnew16, preferred_element_type=F32)

    def affine(t, d):
        ch = chunk_of(t, d)
        pp = jnp.dot(a2_ref[d, ch, c:2 * c, :], uw16_ref[d, ch], preferred_element_type=F32)
        return pp[:, 0:LANES], pp[:, LANES:2 * LANES].astype(BF16)

    def scan_step(t, carry, finish_previous, prepare_next):
        states, vprev, psis, phis = carry[0:2], carry[2:4], carry[4:6], carry[6:8]
        new_states, new_v, new_psi, new_phi = [], [], [], []
        for d in (0, 1):
            ch = chunk_of(t, d)
            st = states[d]
            st16 = st.astype(BF16)
            decay = e3_ref[d, pl.ds(ch, 1), :][:, 0:1]
            new_states.append(st * decay + psis[d]
                              - jnp.dot(phis[d], st16, preferred_element_type=F32))
            r1 = jnp.dot(wq_ref[d, ch], st16, preferred_element_type=F32)
            if finish_previous:
                finish_output(t - 1, d, vprev[d])
            rq_ref[d] = r1[c:2 * c, :]
            new_v.append((u_ref[d, chunk_rows(ch), :] - r1[0:c, :]).astype(BF16))
            psi, phi = affine(t + 1, d) if prepare_next else (psis[d], phis[d])
            new_psi.append(psi)
            new_phi.append(phi)
        return tuple(new_states) + tuple(new_v) + tuple(new_psi) + tuple(new_phi)

    zero_state = jnp.zeros((HEAD_W, HEAD_W), F32)
    zero_v = jnp.zeros((c, HEAD_W), BF16)
    first = [affine(0, d) for d in (0, 1)]
    carry = (zero_state, zero_state, zero_v, zero_v,
             first[0][0], first[1][0], first[0][1], first[1][1])
    if nc > 1:
        carry = scan_step(0, carry, False, True)
        carry = lax.fori_loop(1, nc - 1, lambda t, cr: scan_step(t, cr, True, True), carry,
                              unroll=max(1, min(SCAN_UNROLL, nc - 2)))
    carry = scan_step(nc - 1, carry, nc > 1, False)
    for d in (0, 1):
        finish_output(nc - 1, d, carry[2 + d])

    orows = min(OUT_ROWS, s)

    def out_block(i, carry):
        rows = pl.ds(pl.multiple_of(i * orows, orows), orows)
        o = of_ref[rows, :] + ob_ref[rows, :]
        ms = jnp.mean(o * o, axis=-1, keepdims=True)
        y = o * lax.rsqrt(ms + EPS) * nw_ref[...]
        o_ref[0, rows, :] = (y * _silu(z_ref[0, rows, :])).astype(BF16)
        return carry

    lax.fori_loop(0, s // orows, out_block, 0)


def _gdn(proj3, gates3, conv_w, a_log, dt_bias, norm_w):
    b, s, _ = proj3.shape
    c = GDN_CHUNK
    nc = s // c
    col = lambda off: pl.BlockSpec((1, s, LANES), lambda i, h: (i, 0, off + h))
    cw = lambda off: pl.BlockSpec((CONV_WIDTH, LANES), lambda i, h: (0, off + h))
    smem = pl.BlockSpec(memory_space=pltpu.SMEM)
    return pl.pallas_call(
        _gdn_kernel,
        out_shape=jax.ShapeDtypeStruct((b, s, WIDE), BF16),
        grid=(b, HEADS),
        in_specs=[
            smem, smem,
            col(COL_QA), col(COL_KA), col(COL_VA), col(COL_ZA),
            pl.BlockSpec((GATE_COLS, nc, c), lambda i, h: (0, i, 0)),
            cw(0), cw(HEADS), cw(2 * HEADS),
            pl.BlockSpec((1, LANES), lambda i, h: (0, 0)),
        ],
        out_specs=pl.BlockSpec((1, s, LANES), lambda i, h: (i, 0, h)),
        scratch_shapes=[
            pltpu.VMEM((min(CONV_ROWS, s) + 2 * HALO, LANES), F32),
            pltpu.VMEM((s, LANES), F32),
            pltpu.VMEM((s, LANES), F32),
            pltpu.VMEM((s, LANES), F32),
            pltpu.VMEM((GV_COUNT, nc, c), F32),
            pltpu.VMEM((2, nc, c), F32),
            pltpu.VMEM((2, s, LANES), F32),
            pltpu.VMEM((2, nc, 2 * c, LANES), BF16),
            pltpu.VMEM((2, nc, 2 * c, c), BF16),
            pltpu.VMEM((CHUNK_GROUP, 2, c, 2 * LANES), BF16),
            pltpu.VMEM((2, nc, c, 2 * LANES), BF16),
            pltpu.VMEM((2, c, HEAD_W), F32),
        ],
        compiler_params=pltpu.CompilerParams(
            dimension_semantics=("parallel", "parallel"), vmem_limit_bytes=VMEM_LIMIT),
        name="gdn",
    )(a_log, dt_bias, proj3, proj3, proj3, proj3, gates3, conv_w, conv_w, conv_w, norm_w)


def _outproj_kernel(x_ref, ya_ref, yb_ref, ga_ref, gb_ref, wpa_ref, wpb_ref, wo_ref, fw_ref, o_ref,
                    *, final_norm):
    ta = jnp.dot(ya_ref[...], wpa_ref[...], preferred_element_type=F32)
    tb = jnp.dot(yb_ref[...], wpb_ref[...], preferred_element_type=F32)
    merged = _sigmoid(ga_ref[...]) * ta + _sigmoid(gb_ref[...]) * tb
    xn = x_ref[...] + jnp.dot(merged.astype(BF16), wo_ref[...], preferred_element_type=F32)
    if final_norm:
        ms = jnp.mean(xn * xn, axis=-1, keepdims=True)
        xn = xn * lax.rsqrt(ms + EPS) * fw_ref[...]
    o_ref[...] = xn


def _outproj(x2, ya2, yb2, proj2, w_pa, w_pb, w_out, final_w, final_norm, tm):
    m = x2.shape[0]
    row = lambda blk: pl.BlockSpec((tm, D_MODEL), lambda i: (i, blk))
    wfull = pl.BlockSpec((D_MODEL, D_MODEL), lambda i: (0, 0))
    kern = functools.partial(_outproj_kernel, final_norm=final_norm)
    return pl.pallas_call(
        kern,
        out_shape=jax.ShapeDtypeStruct((m, D_MODEL), F32),
        grid=(m // tm,),
        in_specs=[row(0), row(0), row(0), row(COL_GA // HEADS), row(COL_GB // HEADS),
                  wfull, wfull, wfull, pl.BlockSpec((1, D_MODEL), lambda i: (0, 0))],
        out_specs=row(0),
        compiler_params=pltpu.CompilerParams(
            dimension_semantics=("parallel",), vmem_limit_bytes=VMEM_LIMIT),
        name="outproj",
    )(x2, ya2, yb2, proj2, proj2, w_pa, w_pb, w_out, final_w)


def _pick(n, prefs):
    for p in prefs:
        if n % p == 0:
            return p
    return n


def _rope_tables(positions):
    inv_freq = ROPE_THETA ** (-(jnp.arange(0, ROPE_DIM, 2, dtype=F32) / ROPE_DIM))
    ang = positions.astype(F32)[..., None] * inv_freq
    cos, sin = jnp.cos(ang), jnp.sin(ang)
    half = ROPE_DIM // 2
    b, s, _ = ang.shape
    ones = jnp.ones((b, s, DIFF_DH - ROPE_DIM), F32)
    zeros8 = jnp.zeros((b, s, half), F32)
    zrest = jnp.zeros((b, s, DIFF_DH - ROPE_DIM), F32)
    cos64 = jnp.concatenate([cos, cos, ones], axis=-1)
    sina64 = jnp.concatenate([-sin, zeros8, zrest], axis=-1)
    sinb64 = jnp.concatenate([zeros8, sin, zrest], axis=-1)
    dup = lambda t: jnp.concatenate([t, t], axis=-1)
    return dup(cos64), dup(sina64), dup(sinb64)


def _split_w_in(w_in):
    g0 = 3 * WIDE
    g1 = g0 + GATE_COLS
    seg = lambda k: w_in[:, :, k * WIDE:(k + 1) * WIDE].astype(BF16)
    seg2 = lambda k: w_in[:, :, g1 + k * WIDE:g1 + (k + 1) * WIDE].astype(BF16)
    w_main = jnp.concatenate([seg(0), seg(1), seg(2), seg2(0), seg2(4), seg2(5), seg2(6),
                              seg2(1), seg2(2), seg2(3)], axis=-1)
    wg = jnp.swapaxes(w_in[:, :, g0:g0 + GATE_COLS], 1, 2)
    return w_main, wg.astype(BF16)


def kernel(x, positions, norm_w, w_in, conv_w, a_log, dt_bias, gdn_norm_w, diff_lambda,
           diff_subln_w, w_pa, w_pb, w_out, final_norm_w):
    b, s, d = x.shape
    depth = w_in.shape[0]
    m = b * s
    assert d == D_MODEL and s % GDN_CHUNK == 0 and s % min(CONV_ROWS, s) == 0 and s % min(OUT_ROWS, s) == 0
    tm_in = _pick(m, (2048, 1024, 512, 256))
    tn_in = 512
    tm_out = _pick(m, (512, 256))
    tq = _pick(s, (1024, 512, 256, 128))

    cos_t, sina_t, sinb_t = (t.reshape(m, LANES) for t in _rope_tables(positions))
    w_main, w_gate = _split_w_in(w_in)
    w_pa16, w_pb16, w_out16 = w_pa.astype(BF16), w_pb.astype(BF16), w_out.astype(BF16)

    x2 = x.reshape(m, d)
    for l in range(depth):
        proj2, qk2, vb2, gates2 = _inproj(x2, norm_w[l][None], w_main[l], w_gate[l],
                                          cos_t, sina_t, sinb_t, tm_in, tn_in)
        proj3 = proj2.reshape(b, s, N_F32)
        y_a = _gdn(proj3, gates2.reshape(GATE_COLS, m // GDN_CHUNK, GDN_CHUNK), conv_w[l], a_log[l], dt_bias[l],
                   gdn_norm_w[l][None])
        lambda_init = 0.8 - 0.6 * math.exp(-0.3 * l)
        y_b = _attention(qk2.reshape(b, s, N_ROPE), vb2.reshape(b, s, WIDE), proj3, diff_lambda[l],
                         diff_subln_w[l][None], lambda_init, tq)
        x2 = _outproj(x2, y_a.reshape(m, d), y_b.reshape(m, d), proj2, w_pa16[l], w_pb16[l],
                      w_out16[l], final_norm_w[None], l == depth - 1, tm_out)
    return x2.reshape(b, s, d)
```
